```python
import math
import jax, jax.numpy as jnp
from jax import lax
import numpy as np


D_MODEL = 4096
BATCH = 4
SEQ = 2048
DEPTH = 1

HEAD_DIM = 128
SWA_GROUPS = ((128, 1), (512, 4), (2048, 16))
SWA_HEADS_PER_GROUP = 8
SWA_HEADS = SWA_HEADS_PER_GROUP * 3
SWA_W = SWA_HEADS * HEAD_DIM
SWA_OUT_W = SWA_HEADS_PER_GROUP * HEAD_DIM
SWA_BLOCK = 128
GDN_HEADS = 16
GDN_KEY_DIM = 128
GDN_VAL_DIM = 128
GDN_QK_W = GDN_HEADS * GDN_KEY_DIM
GDN_V_W = GDN_HEADS * GDN_VAL_DIM
GDN_CONV_W = 2 * GDN_QK_W + GDN_V_W
GDN_CONV = 4
GDN_CHUNK = 64
N_EXPERTS = 96
TOP_K = 8
D_EXPERT = 512
D_SHARED = 512
ROUTED_SCALE = 2.5
MOE_BLOCK = 128
LN_EPS = 1e-5
RMS_EPS = 1e-6
DEEPNORM_ALPHA = (2 * DEPTH) ** 0.25
DEEPNORM_BETA = (8 * DEPTH) ** -0.25

kernel_name = 'hybrid_dilated_swa_gdn_moe_deepnorm'


def layer_norm(x, g, b):
    xf = x.astype(jnp.float32)
    mu = jnp.mean(xf, -1, keepdims=True)
    var = jnp.mean(jnp.square(xf - mu), -1, keepdims=True)
    return ((xf - mu) * lax.rsqrt(var + LN_EPS) * g + b).astype(x.dtype)


def l2_normalize(t):
    return t * lax.rsqrt(jnp.sum(jnp.square(t), -1, keepdims=True) + 1e-6)


def causal_depthwise_conv(x, w):
    k = w.shape[0]
    return lax.conv_general_dilated(x, w.astype(x.dtype), window_strides=(1,), padding=((k - 1, 0),),
                                    dimension_numbers=('NWC', 'WIO', 'NWC'), feature_group_count=x.shape[-1])


def dilated_window_attention(q, k, v, window, dilation):
    B, S, H, Dh = q.shape
    span = window // dilation
    n_prev = -(-span // SWA_BLOCK)
    L = S // dilation
    nb = -(-L // SWA_BLOCK)
    lp = nb * SWA_BLOCK

    def to_blocks(t):
        t = t.reshape(B, L, dilation, H, Dh).transpose(0, 2, 3, 1, 4)
        t = jnp.pad(t, ((0, 0), (0, 0), (0, 0), (0, lp - L), (0, 0)))
        return t.reshape(B, dilation, H, nb, SWA_BLOCK, Dh)

    def with_prev(t):
        tp = jnp.pad(t, ((0, 0), (0, 0), (0, 0), (n_prev, 0), (0, 0), (0, 0)))
        return jnp.concatenate([tp[:, :, :, i:i + nb] for i in range(n_prev + 1)], axis=4)

    qb = to_blocks(q)
    kw = with_prev(to_blocks(k))
    vw = with_prev(to_blocks(v))
    s = jnp.einsum('bdhnqc,bdhnkc->bdhnqk', qb, kw, preferred_element_type=jnp.float32) * (Dh ** -0.5)
    qpos = jnp.arange(nb)[:, None] * SWA_BLOCK + jnp.arange(SWA_BLOCK)[None, :]
    kpos = (jnp.arange(nb)[:, None] - n_prev) * SWA_BLOCK + jnp.arange((n_prev + 1) * SWA_BLOCK)[None, :]
    dist = qpos[:, :, None] - kpos[:, None, :]
    mask = (dist >= 0) & (dist <= span) & (kpos[:, None, :] >= 0)
    s = jnp.where(mask, s, -jnp.inf)
    m = jnp.max(s, -1, keepdims=True)
    p = jnp.exp(s - m)
    den = jnp.sum(p, -1, keepdims=True)
    o = jnp.einsum('bdhnqk,bdhnkc->bdhnqc', p / den, vw.astype(jnp.float32))
    lse = (m + jnp.log(den))[..., 0]
    o = o.reshape(B, dilation, H, lp, Dh)[:, :, :, :L].transpose(0, 3, 1, 2, 4).reshape(B, S, H, Dh)
    lse = lse.reshape(B, dilation, H, lp)[..., :L].transpose(0, 3, 1, 2).reshape(B, S, H)
    return o, lse


def gated_delta_rule(q, k, v, g, beta):
    f32 = jnp.float32
    B, S, H, Dk = q.shape
    Dv = v.shape[-1]
    C = GDN_CHUNK
    N = S // C
    q = l2_normalize(q.astype(f32)) * (Dk ** -0.5)
    k = l2_normalize(k.astype(f32))
    v = v.astype(f32)

    def chunks(t):
        return t.reshape(B, N, C, H, -1).transpose(0, 3, 1, 2, 4)

    q, k, v = chunks(q), chunks(k), chunks(v)
    beta = beta.reshape(B, N, C, H).transpose(0, 3, 1, 2)
    G = jnp.cumsum(g.reshape(B, N, C, H).transpose(0, 3, 1, 2), axis=-1)
    tri = jnp.tril(jnp.ones((C, C), bool))
    strict = jnp.tril(jnp.ones((C, C), bool), -1)
    gam = jnp.exp(jnp.where(tri, G[..., :, None] - G[..., None, :], -jnp.inf))
    kb = k * beta[..., None]
    lmat = jnp.where(strict, jnp.einsum('bhnic,bhnjc->bhnij', kb, k) * gam, 0.0)
    rhs = jnp.concatenate([v * beta[..., None], kb * jnp.exp(G)[..., None]], axis=-1)
    sol = lax.linalg.triangular_solve(jnp.eye(C, dtype=f32) + lmat, rhs, left_side=True, lower=True,
                                      unit_diagonal=True)
    u, w = sol[..., :Dv], sol[..., Dv:]
    qk = jnp.where(tri, jnp.einsum('bhnic,bhnjc->bhnij', q, k) * gam, 0.0)
    qg = q * jnp.exp(G)[..., None]
    kd = k * jnp.exp(G[..., -1:] - G)[..., None]
    gl = jnp.exp(G[..., -1])

    def step(state, inp):
        qg_n, kd_n, u_n, w_n, qk_n, gl_n = inp
        vnew = u_n - jnp.einsum('bhck,bhkv->bhcv', w_n, state)
        o = jnp.einsum('bhck,bhkv->bhcv', qg_n, state) + jnp.einsum('bhij,bhjv->bhiv', qk_n, vnew)
        state = state * gl_n[..., None, None] + jnp.einsum('bhck,bhcv->bhkv', kd_n, vnew)
        return state, o

    xs = tuple(jnp.moveaxis(t, 2, 0) for t in (qg, kd, u, w, qk, gl))
    _, o = lax.scan(step, jnp.zeros((B, H, Dk, Dv), f32), xs)
    return o.transpose(1, 0, 3, 2, 4).reshape(B, S, H, Dv)


def token_mixers(x, w_in, conv_w, a_log, dt_bias, norm_g, w_o_swa, w_o_gdn, w_out):
    f32 = jnp.float32
    B, S, _ = x.shape
    widths = (SWA_W, SWA_W, SWA_W, GDN_CONV_W, GDN_V_W, GDN_HEADS, GDN_HEADS, D_MODEL)
    cuts, acc = [], 0
    for wd in widths:
        acc += wd
        cuts.append(acc)
    qa, ka, va, qkv_b, z_b, b_b, a_b, g_a, g_b = jnp.split(x @ w_in, cuts, axis=-1)

    qa = qa.reshape(B, S, SWA_HEADS, HEAD_DIM)
    ka = ka.reshape(B, S, SWA_HEADS, HEAD_DIM)
    va = va.reshape(B, S, SWA_HEADS, HEAD_DIM)
    outs, lses = [], []
    for j, (window, dilation) in enumerate(SWA_GROUPS):
        hs = slice(j * SWA_HEADS_PER_GROUP, (j + 1) * SWA_HEADS_PER_GROUP)
        o, lse = dilated_window_attention(qa[:, :, hs], ka[:, :, hs], va[:, :, hs], window, dilation)
        outs.append(o)
        lses.append(lse)
    mix_w = jax.nn.softmax(jnp.stack(lses), axis=0)
    y_a = jnp.sum(mix_w[..., None] * jnp.stack(outs), axis=0).reshape(B, S, SWA_OUT_W).astype(x.dtype)

    qkv_b = jax.nn.silu(causal_depthwise_conv(qkv_b, conv_w))
    q_b, k_b, v_b = jnp.split(qkv_b, (GDN_QK_W, 2 * GDN_QK_W), axis=-1)
    g = -jnp.exp(a_log.astype(f32)) * jax.nn.softplus(a_b.astype(f32) + dt_bias.astype(f32))
    beta = jax.nn.sigmoid(b_b.astype(f32))
    o_b = gated_delta_rule(q_b.reshape(B, S, GDN_HEADS, GDN_KEY_DIM), k_b.reshape(B, S, GDN_HEADS, GDN_KEY_DIM),
                           v_b.reshape(B, S, GDN_HEADS, GDN_VAL_DIM), g, beta)
    o_b = o_b * lax.rsqrt(jnp.mean(jnp.square(o_b), -1, keepdims=True) + RMS_EPS) * norm_g.astype(f32)
    o_b = o_b * jax.nn.silu(z_b.astype(f32).reshape(B, S, GDN_HEADS, GDN_VAL_DIM))
    y_b = o_b.reshape(B, S, GDN_V_W).astype(x.dtype)

    merged = jax.nn.sigmoid(g_a) * (y_a @ w_o_swa) + jax.nn.sigmoid(g_b) * (y_b @ w_o_gdn)
    return merged @ w_out


def moe_ffn(h, w_router, router_bias, w_e_gate, w_e_up, w_e_down, w_s_gate, w_s_up, w_s_down):
    B, S, D = h.shape
    T = B * S
    xf = h.reshape(T, D)
    scores = jax.nn.sigmoid(jnp.dot(xf, w_router, preferred_element_type=jnp.float32))
    _, idx = lax.top_k(scores + router_bias.astype(jnp.float32), TOP_K)
    sel = jnp.take_along_axis(scores, idx, axis=-1)
    gate = sel / jnp.sum(sel, -1, keepdims=True) * ROUTED_SCALE
    flat_e = idx.reshape(-1)
    flat_tok = jnp.arange(T * TOP_K, dtype=jnp.int32) // TOP_K
    order = jnp.argsort(flat_e)
    se = flat_e[order]
    counts = jnp.bincount(flat_e, length=N_EXPERTS)
    padded = (counts + MOE_BLOCK - 1) // MOE_BLOCK * MOE_BLOCK
    start = jnp.cumsum(counts) - counts
    pend = jnp.cumsum(padded)
    pstart = pend - padded
    dest = pstart[se] + jnp.arange(T * TOP_K, dtype=jnp.int32) - start[se]
    n_blocks = (T * TOP_K + N_EXPERTS * (MOE_BLOCK - 1) + MOE_BLOCK - 1) // MOE_BLOCK
    rows = n_blocks * MOE_BLOCK
    row_tok = jnp.full((rows,), T, jnp.int32).at[dest].set(flat_tok[order])
    row_w = jnp.zeros((rows,), jnp.float32).at[dest].set(gate.reshape(-1)[order])
    block_e = jnp.minimum(jnp.searchsorted(pend, jnp.arange(n_blocks, dtype=jnp.int32) * MOE_BLOCK, side='right'),
                          N_EXPERTS - 1)
    x_pad = jnp.concatenate([xf, jnp.zeros((1, D), xf.dtype)], axis=0)

    def block_step(acc, blk):
        tok, wt, e = blk
        xb = x_pad[tok]
        hb = jax.nn.silu(xb @ w_e_gate[e]) * (xb @ w_e_up[e])
        yb = (hb @ w_e_down[e]) * wt[:, None].astype(xb.dtype)
        return acc.at[tok].add(yb), None

    routed, _ = lax.scan(block_step, jnp.zeros((T + 1, D), h.dtype),
                         (row_tok.reshape(n_blocks, MOE_BLOCK), row_w.reshape(n_blocks, MOE_BLOCK), block_e))
    shared = (jax.nn.silu(xf @ w_s_gate) * (xf @ w_s_up)) @ w_s_down
    return (routed[:T] + shared).reshape(B, S, D)


def setup_inputs(seed: int = 0) -> dict:
    key = jax.random.key(seed)
    ks = jax.random.split(key, 25)
    L = DEPTH
    D = D_MODEL

    def nrm(k, shape, scale):
        return jax.random.normal(k, shape, jnp.float32) * scale

    fan = D ** -0.5
    x = nrm(ks[0], (BATCH, SEQ, D), 1.0)
    w_in = jnp.concatenate([
        nrm(ks[1], (L, D, 2 * SWA_W), fan),
        nrm(ks[2], (L, D, SWA_W), fan * DEEPNORM_BETA),
        nrm(ks[3], (L, D, 2 * GDN_QK_W), fan),
        nrm(ks[4], (L, D, GDN_V_W), fan * DEEPNORM_BETA),
        nrm(ks[5], (L, D, GDN_V_W + 2 * GDN_HEADS + 2 * D), fan)], axis=-1)
    conv_w = nrm(ks[6], (L, GDN_CONV, 1, GDN_CONV_W), GDN_CONV ** -0.5)
    gdn_a_log = jnp.log(jax.random.uniform(ks[7], (L, GDN_HEADS), jnp.float32, 1.0, 16.0))
    dt = jnp.exp(jax.random.uniform(ks[8], (L, GDN_HEADS), jnp.float32, math.log(1e-3), math.log(1e-1)))
    gdn_dt_bias = dt + jnp.log(-jnp.expm1(-dt))
    gdn_norm_g = 1.0 + nrm(ks[9], (L, GDN_VAL_DIM), 0.02)
    w_o_swa = nrm(ks[10], (L, SWA_OUT_W, D), SWA_OUT_W ** -0.5)
    w_o_gdn = nrm(ks[11], (L, GDN_V_W, D), GDN_V_W ** -0.5)
    w_out = nrm(ks[12], (L, D, D), fan * DEEPNORM_BETA)
    ln1_g = 1.0 + nrm(ks[13], (L, D), 0.02)
    ln1_b = nrm(ks[14], (L, D), 0.02)
    w_router = nrm(ks[15], (L, D, N_EXPERTS), fan)
    router_bias = nrm(ks[16], (L, N_EXPERTS), 0.01)
    w_e_gate = nrm(ks[17], (L, N_EXPERTS, D, D_EXPERT), fan * DEEPNORM_BETA)
    w_e_up = nrm(ks[18], (L, N_EXPERTS, D, D_EXPERT), fan * DEEPNORM_BETA)
    w_e_down = nrm(ks[19], (L, N_EXPERTS, D_EXPERT, D), D_EXPERT ** -0.5 * DEEPNORM_BETA)
    w_s_gate = nrm(ks[20], (L, D, D_SHARED), fan * DEEPNORM_BETA)
    w_s_up = nrm(ks[21], (L, D, D_SHARED), fan * DEEPNORM_BETA)
    w_s_down = nrm(ks[22], (L, D_SHARED, D), D_SHARED ** -0.5 * DEEPNORM_BETA)
    ln2_g = 1.0 + nrm(ks[23], (L, D), 0.02)
    ln2_b = nrm(ks[24], (L, D), 0.02)
    return {'x': x, 'w_in': w_in, 'conv_w': conv_w, 'gdn_a_log': gdn_a_log, 'gdn_dt_bias': gdn_dt_bias,
            'gdn_norm_g': gdn_norm_g, 'w_o_swa': w_o_swa, 'w_o_gdn': w_o_gdn, 'w_out': w_out,
            'ln1_g': ln1_g, 'ln1_b': ln1_b, 'w_router': w_router, 'router_bias': router_bias,
            'w_e_gate': w_e_gate, 'w_e_up': w_e_up, 'w_e_down': w_e_down, 'w_s_gate': w_s_gate,
            'w_s_up': w_s_up, 'w_s_down': w_s_down, 'ln2_g': ln2_g, 'ln2_b': ln2_b}


def reference(x, w_in, conv_w, gdn_a_log, gdn_dt_bias, gdn_norm_g, w_o_swa, w_o_gdn, w_out, ln1_g, ln1_b,
              w_router, router_bias, w_e_gate, w_e_up, w_e_down, w_s_gate, w_s_up, w_s_down, ln2_g, ln2_b):
    for l in range(DEPTH):
        mix = token_mixers(x, w_in[l], conv_w[l], gdn_a_log[l], gdn_dt_bias[l], gdn_norm_g[l],
                           w_o_swa[l], w_o_gdn[l], w_out[l])
        h = layer_norm(DEEPNORM_ALPHA * x + mix, ln1_g[l], ln1_b[l])
        ffn = moe_ffn(h, w_router[l], router_bias[l], w_e_gate[l], w_e_up[l], w_e_down[l],
                      w_s_gate[l], w_s_up[l], w_s_down[l])
        x = layer_norm(DEEPNORM_ALPHA * h + ffn, ln2_g[l], ln2_b[l])
    return x
```

```python
import functools
import math

import jax
import jax.numpy as jnp
from jax import lax
from jax.experimental import pallas as pl
from jax.experimental.pallas import tpu as pltpu

HEAD_DIM = 128
SWA_GROUPS = ((128, 1), (512, 4), (2048, 16))
SWA_HEADS_PER_GROUP = 8
GDN_HEADS = 16
GDN_CONV = 4
GDN_CHUNK = 64
N_EXPERTS = 96
TOP_K = 8
ROUTED_SCALE = 2.5
LN_EPS = 1e-5
RMS_EPS = 1e-6
DEPTH = 1
DEEPNORM_ALPHA = (2 * DEPTH) ** 0.25

LANES = 128
SUBLANES = 8
VMEM_LIMIT = 56 * 1024 * 1024
MOE_ROWS = 256

F32 = jnp.float32
BF16 = jnp.bfloat16
HIGHEST = lax.Precision.HIGHEST


def _cparams(sem):
    return pltpu.CompilerParams(dimension_semantics=sem, vmem_limit_bytes=VMEM_LIMIT)


def _dot(a, b):
    return jnp.dot(a, b, preferred_element_type=F32)


def _dot_nt(a, b):
    return lax.dot_general(a, b, (((1,), (1,)), ((), ())), preferred_element_type=F32)


def _dot_tn(a, b):
    return lax.dot_general(a, b, (((0,), (0,)), ((), ())), preferred_element_type=F32)


def _dot_hi(a, b):
    return jnp.dot(a, b, preferred_element_type=F32, precision=HIGHEST)


def _sigmoid(x):
    return 1.0 / (1.0 + jnp.exp(-x))


def _silu(x):
    return x * _sigmoid(x)


def _proj_kernel(a_ref, b_ref, o_ref):
    o_ref[...] = _dot(a_ref[...].astype(BF16), b_ref[...].astype(BF16)).astype(o_ref.dtype)


def _proj(a, b, n_cols, *, tm, tn, out_dtype, col_off_blocks=0, name="proj"):
    m, k = a.shape
    assert m % tm == 0 and n_cols % tn == 0
    return pl.pallas_call(
        _proj_kernel,
        grid=(m // tm, n_cols // tn),
        in_specs=[pl.BlockSpec((tm, k), lambda i, j: (i, 0)),
                  pl.BlockSpec((k, tn), lambda i, j: (0, j + col_off_blocks))],
        out_specs=pl.BlockSpec((tm, tn), lambda i, j: (i, j)),
        out_shape=jax.ShapeDtypeStruct((m, n_cols), out_dtype),
        compiler_params=_cparams(("parallel", "arbitrary")),
        name=name,
    )(a, b)


def _swa_kernel(q_ref, k_ref, v_ref, o_ref, lse_ref, *, n_heads, span, n_blocks):
    scale = HEAD_DIM ** -0.5
    qi = lax.broadcasted_iota(jnp.int32, (span, 2 * span), 0)
    ci = lax.broadcasted_iota(jnp.int32, (span, 2 * span), 1)
    mask_win = (ci >= qi) & (ci <= qi + span)
    qi0 = lax.broadcasted_iota(jnp.int32, (span, span), 0)
    ci0 = lax.broadcasted_iota(jnp.int32, (span, span), 1)
    mask_first = ci0 <= qi0
    lane = lax.broadcasted_iota(jnp.int32, (span, LANES), 1)

    def attend(q, kw, vw, mask):
        s = _dot_nt(q, kw) * scale
        s = jnp.where(mask, s, -jnp.inf)
        m = jnp.max(s, axis=-1, keepdims=True)
        p = jnp.exp(s - m)
        den = jnp.sum(p, axis=-1, keepdims=True)
        o = _dot(p.astype(BF16), vw) / den
        return o, m + jnp.log(den)

    def block(q0, k0, klen, mask):
        lse_acc = jnp.zeros((span, LANES), F32)
        for h in range(n_heads):
            cs = slice(h * HEAD_DIM, (h + 1) * HEAD_DIM)
            o, lse = attend(q_ref[pl.ds(q0, span), cs], k_ref[pl.ds(k0, klen), cs],
                            v_ref[pl.ds(k0, klen), cs], mask)
            o_ref[pl.ds(q0, span), cs] = o.astype(o_ref.dtype)
            lse_acc = jnp.where(lane == h, lse, lse_acc)
        lse_ref[pl.ds(q0, span), :] = lse_acc

    block(0, 0, span, mask_first)

    def body(n, carry):
        q0 = pl.multiple_of(n * span, span)
        k0 = pl.multiple_of((n - 1) * span, span)
        block(q0, k0, 2 * span, mask_win)
        return carry

    lax.fori_loop(1, n_blocks, body, 0)


def _swa_group(proj3, group, batch, seq, width):
    window, dil = SWA_GROUPS[group]
    span = window // dil
    sub_len = seq // dil
    assert seq % dil == 0 and sub_len % span == 0 and span % LANES == 0
    gw = SWA_HEADS_PER_GROUP * HEAD_DIM
    n_groups = len(SWA_GROUPS)
    assert width % gw == 0
    cb = width // gw
    view = proj3.reshape(batch, sub_len, dil * width)
    kern = functools.partial(_swa_kernel, n_heads=SWA_HEADS_PER_GROUP, span=span,
                             n_blocks=sub_len // span)

    def spec(seg):
        return pl.BlockSpec((None, sub_len, gw), lambda b, r: (b, 0, r * cb + seg * n_groups + group))

    o, lse = pl.pallas_call(
        kern,
        grid=(batch, dil),
        in_specs=[spec(0), spec(1), spec(2)],
        out_specs=[pl.BlockSpec((None, sub_len, gw), lambda b, r: (b, 0, r)),
                   pl.BlockSpec((None, sub_len, LANES), lambda b, r: (b, 0, r))],
        out_shape=[jax.ShapeDtypeStruct((batch, sub_len, dil * gw), BF16),
                   jax.ShapeDtypeStruct((batch, sub_len, dil * LANES), F32)],
        compiler_params=_cparams(("parallel", "parallel")),
        name=f"swa_g{group}",
    )(view, view, view)
    return o.reshape(batch * seq, gw), lse.reshape(batch * seq, LANES)


def _gdn_kernel(alog_ref, dtb_ref, q_ref, k_ref, v_ref, z_ref, cwq_ref, cwk_ref, cwv_ref,
                a_ref, b_ref, ng_ref, o_ref,
                xpad, qn, kn, vn, qg_s, kd_s, w_s, u_s, qk_s, gl_s, st_s, *, hb, seq):
    C = GDN_CHUNK
    D = HEAD_DIM
    n_chunks = seq // C
    hg = pl.program_id(1)
    pad = SUBLANES

    xpad[0:pad, :] = jnp.zeros((pad, hb * D), F32)
    rows = 256
    for src_ref, cw_ref, dst, mode in ((q_ref, cwq_ref, qn, "q"), (k_ref, cwk_ref, kn, "k"),
                                       (v_ref, cwv_ref, vn, "v")):
        xpad[pad:pad + seq, :] = src_ref[...].astype(F32)
        cw = cw_ref[...]
        for t0 in range(0, seq, rows):
            acc = xpad[pad + t0:pad + t0 + rows, :] * cw[GDN_CONV - 1:GDN_CONV, :]
            for j in range(GDN_CONV - 1):
                sh = GDN_CONV - 1 - j
                acc = acc + xpad[pad + t0 - sh:pad + t0 - sh + rows, :] * cw[j:j + 1, :]
            y = _silu(acc)
            for hh in range(hb):
                cs = slice(hh * D, (hh + 1) * D)
                yh = y[:, cs]
                if mode != "v":
                    yh = yh * lax.rsqrt(jnp.sum(yh * yh, axis=-1, keepdims=True) + 1e-6)
                if mode == "q":
                    yh = yh * (D ** -0.5)
                dst[t0:t0 + rows, cs] = yh

    ri = lax.broadcasted_iota(jnp.int32, (C, C), 0)
    ci = lax.broadcasted_iota(jnp.int32, (C, C), 1)
    tri = ri >= ci
    strict = ri > ci
    eye = ri == ci
    upper_ones = (ri <= ci).astype(F32)
    eye_f = eye.astype(F32)

    def chunk_body(n, carry):
        r0 = pl.multiple_of(n * C, C)
        for hh in range(hb):
            head = hg * hb + hh
            cs = slice(hh * D, (hh + 1) * D)
            a_row = a_ref[hh, pl.ds(n, 1), :]
            b_row = b_ref[hh, pl.ds(n, 1), :]
            sp_in = a_row + dtb_ref[head]
            softplus = jnp.maximum(sp_in, 0.0) + jnp.log(1.0 + jnp.exp(-jnp.abs(sp_in)))
            g_row = -jnp.exp(alog_ref[head]) * softplus
            beta_row = _sigmoid(b_row)
            g_cum_r = _dot_hi(jnp.broadcast_to(g_row, (C, C)), upper_ones)
            g_cum_c = jnp.sum(jnp.where(eye, g_cum_r, 0.0), axis=1, keepdims=True)
            beta_c = jnp.sum(jnp.where(eye, jnp.broadcast_to(beta_row, (C, C)), 0.0), axis=1,
                             keepdims=True)
            g_last = g_cum_r[:, C - 1:C]
            gam = jnp.exp(jnp.where(tri, g_cum_c - g_cum_r, -jnp.inf))
            q = qn[pl.ds(r0, C), cs]
            k = kn[pl.ds(r0, C), cs]
            v = vn[pl.ds(r0, C), cs]
            kb = k * beta_c
            k16 = k.astype(BF16)
            neg_l = -jnp.where(strict, _dot_nt(kb.astype(BF16), k16) * gam, 0.0)
            t_inv = eye_f + neg_l
            pw = neg_l
            for _ in range(int(math.log2(C)) - 1):
                pw = _dot_hi(pw, pw)
                t_inv = t_inv + _dot_hi(t_inv, pw)
            e_g = jnp.exp(g_cum_c)
            rhs = jnp.concatenate([v * beta_c, kb * e_g], axis=1)
            sol = _dot_hi(t_inv, rhs)
            qk = jnp.where(tri, _dot_nt(q.astype(BF16), k16) * gam, 0.0)
            u_s[pl.ds(r0, C), cs] = sol[:, :D]
            w_s[pl.ds(r0, C), cs] = sol[:, D:].astype(BF16)
            qg_s[pl.ds(r0, C), cs] = (q * e_g).astype(BF16)
            kd_s[pl.ds(r0, C), cs] = (k * jnp.exp(g_last - g_cum_c)).astype(BF16)
            qk_s[hh, pl.ds(r0, C), :] = qk.astype(BF16)
            gl_s[hh, pl.ds(pl.multiple_of(n * SUBLANES, SUBLANES), SUBLANES), :] = jnp.broadcast_to(
                jnp.exp(g_last[0:SUBLANES, :]), (SUBLANES, D))
        return carry

    lax.fori_loop(0, n_chunks, chunk_body, 0)

    st_s[...] = jnp.zeros(st_s.shape, F32)
    ng = ng_ref[...]

    def rec_body(n, carry):
        r0 = pl.multiple_of(n * C, C)
        for hh in range(hb):
            cs = slice(hh * D, (hh + 1) * D)
            state = st_s[hh]
            s16 = state.astype(BF16)
            vnew = u_s[pl.ds(r0, C), cs] - _dot(w_s[pl.ds(r0, C), cs], s16)
            v16 = vnew.astype(BF16)
            o = _dot(qg_s[pl.ds(r0, C), cs], s16) + _dot(qk_s[hh, pl.ds(r0, C), :], v16)
            gl = gl_s[hh, pl.ds(pl.multiple_of(n * SUBLANES, SUBLANES), 1), :]
            st_s[hh] = state * gl + _dot_tn(kd_s[pl.ds(r0, C), cs], v16)
            o = o * lax.rsqrt(jnp.mean(o * o, axis=-1, keepdims=True) + RMS_EPS) * ng
            z = z_ref[pl.ds(r0, C), cs].astype(F32)
            o_ref[pl.ds(r0, C), cs] = (o * _silu(z)).astype(o_ref.dtype)
        return carry

    lax.fori_loop(0, n_chunks, rec_body, 0)


def _gdn(proj3, conv_w2, a_t, b_t, a_log, dt_bias, norm_g, batch, seq, col0, hb=2):
    C = GDN_CHUNK
    H = GDN_HEADS
    D = HEAD_DIM
    bw = hb * D
    assert H % hb == 0 and col0 % bw == 0 and seq % 256 == 0
    n_hg = H // hb
    c0 = col0 // bw
    nq = H * D // bw
    n_chunks = seq // C

    def pspec(seg):
        return pl.BlockSpec((None, seq, bw), lambda b, g, *_: (b, 0, c0 + seg * nq + g))

    def cwspec(seg):
        return pl.BlockSpec((GDN_CONV, bw), lambda b, g, *_: (0, seg * nq + g))

    abspec = pl.BlockSpec((None, hb, n_chunks, C), lambda b, g, *_: (b, g, 0, 0))
    kern = functools.partial(_gdn_kernel, hb=hb, seq=seq)
    return pl.pallas_call(
        kern,
        grid_spec=pltpu.PrefetchScalarGridSpec(
            num_scalar_prefetch=2,
            grid=(batch, n_hg),
            in_specs=[pspec(0), pspec(1), pspec(2), pspec(3), cwspec(0), cwspec(1), cwspec(2),
                      abspec, abspec, pl.BlockSpec((1, D), lambda b, g, *_: (0, 0))],
            out_specs=pl.BlockSpec((None, seq, bw), lambda b, g, *_: (b, 0, g)),
            scratch_shapes=[
                pltpu.VMEM((seq + SUBLANES, bw), F32),
                pltpu.VMEM((seq, bw), F32),
                pltpu.VMEM((seq, bw), F32),
                pltpu.VMEM((seq, bw), F32),
                pltpu.VMEM((seq, bw), BF16),
                pltpu.VMEM((seq, bw), BF16),
                pltpu.VMEM((seq, bw), BF16),
                pltpu.VMEM((seq, bw), F32),
                pltpu.VMEM((hb, seq, C), BF16),
                pltpu.VMEM((hb, n_chunks * SUBLANES, D), F32),
                pltpu.VMEM((hb, D, D), F32),
            ]),
        out_shape=jax.ShapeDtypeStruct((batch, seq, H * D), BF16),
        compiler_params=_cparams(("parallel", "parallel")),
        name="gdn",
    )(a_log, dt_bias, proj3, proj3, proj3, proj3, conv_w2, conv_w2, conv_w2, a_t, b_t, norm_g)


def _merge_kernel(o0_ref, o1_ref, o2_ref, l0_ref, l1_ref, l2_ref, yb_ref, ga_ref, gb_ref,
                  wa_ref, wb_ref, out_ref, ya_s):
    @pl.when(pl.program_id(1) == 0)
    def _():
        l0, l1, l2 = l0_ref[...], l1_ref[...], l2_ref[...]
        m = jnp.maximum(jnp.maximum(l0, l1), l2)
        e0, e1, e2 = jnp.exp(l0 - m), jnp.exp(l1 - m), jnp.exp(l2 - m)
        inv = 1.0 / (e0 + e1 + e2)
        w0, w1, w2 = e0 * inv, e1 * inv, e2 * inv
        for h in range(SWA_HEADS_PER_GROUP):
            cs = slice(h * HEAD_DIM, (h + 1) * HEAD_DIM)
            ya = (w0[:, h:h + 1] * o0_ref[:, cs].astype(F32) + w1[:, h:h + 1] * o1_ref[:, cs].astype(F32)
                  + w2[:, h:h + 1] * o2_ref[:, cs].astype(F32))
            ya_s[:, cs] = ya.astype(BF16)

    pa = _dot(ya_s[...], wa_ref[...])
    pb = _dot(yb_ref[...], wb_ref[...])
    out_ref[...] = (_sigmoid(ga_ref[...].astype(F32)) * pa
                    + _sigmoid(gb_ref[...].astype(F32)) * pb).astype(out_ref.dtype)


def _merge(o_groups, lse_groups, y_b, gates, w_o_swa, w_o_gdn, d_model, *, tm=512, tn=512):
    t = y_b.shape[0]
    wa_k, wb_k = w_o_swa.shape[0], w_o_gdn.shape[0]
    assert t % tm == 0 and d_model % tn == 0
    nb = d_model // tn
    row = lambda w: pl.BlockSpec((tm, w), lambda i, j: (i, 0))
    return pl.pallas_call(
        _merge_kernel,
        grid=(t // tm, nb),
        in_specs=[row(wa_k), row(wa_k), row(wa_k), row(LANES), row(LANES), row(LANES), row(wb_k),
                  pl.BlockSpec((tm, tn), lambda i, j: (i, j)),
                  pl.BlockSpec((tm, tn), lambda i, j: (i, nb + j)),
                  pl.BlockSpec((wa_k, tn), lambda i, j: (0, j)),
                  pl.BlockSpec((wb_k, tn), lambda i, j: (0, j))],
        out_specs=pl.BlockSpec((tm, tn), lambda i, j: (i, j)),
        out_shape=jax.ShapeDtypeStruct((t, d_model), BF16),
        scratch_shapes=[pltpu.VMEM((tm, wa_k), BF16)],
        compiler_params=_cparams(("parallel", "arbitrary")),
        name="merge",
    )(*o_groups, *lse_groups, y_b, gates, gates, w_o_swa, w_o_gdn)


def _layer_norm(v, g, b):
    mu = jnp.mean(v, axis=-1, keepdims=True)
    c = v - mu
    var = jnp.mean(c * c, axis=-1, keepdims=True)
    return c * lax.rsqrt(var + LN_EPS) * g + b


def _proj_ln_kernel(a_ref, w_ref, res_ref, g_ref, b_ref, o_ref, acc):
    k = pl.program_id(1)

    @pl.when(k == 0)
    def _():
        acc[...] = jnp.zeros(acc.shape, F32)

    acc[...] += _dot(a_ref[...], w_ref[...])

    @pl.when(k == pl.num_programs(1) - 1)
    def _():
        o_ref[...] = _layer_norm(DEEPNORM_ALPHA * res_ref[...] + acc[...], g_ref[...], b_ref[...])


def _proj_ln(a, w, res, g, b, *, tm=256, tk=512):
    t, kdim = a.shape
    d = w.shape[1]
    assert t % tm == 0 and kdim % tk == 0
    return pl.pallas_call(
        _proj_ln_kernel,
        grid=(t // tm, kdim // tk),
        in_specs=[pl.BlockSpec((tm, tk), lambda i, k: (i, k)),
                  pl.BlockSpec((tk, d), lambda i, k: (k, 0)),
                  pl.BlockSpec((tm, d), lambda i, k: (i, 0)),
                  pl.BlockSpec((1, d), lambda i, k: (0, 0)),
                  pl.BlockSpec((1, d), lambda i, k: (0, 0))],
        out_specs=pl.BlockSpec((tm, d), lambda i, k: (i, 0)),
        out_shape=jax.ShapeDtypeStruct((t, d), F32),
        scratch_shapes=[pltpu.VMEM((tm, d), F32)],
        compiler_params=_cparams(("parallel", "arbitrary")),
        name="proj_ln",
    )(a, w, res, g, b)


def _router_kernel(h_ref, w_ref, bias_ref, eidx_ref, gate_ref, pos_ref, cnt_ref, cnt_s, *, tm):
    i = pl.program_id(0)

    @pl.when(i == 0)
    def _():
        cnt_s[...] = jnp.zeros(cnt_s.shape, F32)

    scores = _sigmoid(_dot_hi(h_ref[...], w_ref[...]))
    lane = lax.broadcasted_iota(jnp.int32, (tm, LANES), 1)
    sel = jnp.where(lane < N_EXPERTS, scores + bias_ref[...], -jnp.inf)
    chosen = jnp.zeros((tm, LANES), jnp.bool_)
    eidx = jnp.zeros((tm, LANES), jnp.int32)
    gsel = jnp.zeros((tm, LANES), F32)
    picks = []
    for k in range(TOP_K):
        m = jnp.max(sel, axis=-1, keepdims=True)
        idx = jnp.min(jnp.where(sel == m, lane, LANES), axis=-1, keepdims=True)
        picks.append(idx)
        hit = lane == idx
        sc = jnp.sum(jnp.where(hit, scores, 0.0), axis=-1, keepdims=True)
        eidx = jnp.where(lane == k, idx, eidx)
        gsel = jnp.where(lane == k, sc, gsel)
        chosen = chosen | hit
        sel = jnp.where(hit, -jnp.inf, sel)
    gate_ref[...] = gsel / jnp.sum(gsel, axis=-1, keepdims=True) * ROUTED_SCALE

    ri = lax.broadcasted_iota(jnp.int32, (tm, tm), 0)
    ci = lax.broadcasted_iota(jnp.int32, (tm, tm), 1)
    ch = jnp.where(chosen, 1.0, 0.0)
    rank = _dot(jnp.where(ri > ci, 1.0, 0.0).astype(BF16), ch.astype(BF16)) + cnt_s[0:1, :]
    pos = jnp.zeros((tm, LANES), F32)
    for k in range(TOP_K):
        pk = jnp.sum(jnp.where(lane == picks[k], rank, 0.0), axis=-1, keepdims=True)
        pos = jnp.where(lane == k, pk, pos)
    pos_ref[...] = pos.astype(jnp.int32)
    eidx_ref[...] = eidx
    total = cnt_s[...] + jnp.sum(ch, axis=0, keepdims=True)
    cnt_s[...] = total
    cnt_ref[...] = total


def _router(h, w_router_p, bias_p, *, tm=256):
    t, d = h.shape
    assert t % tm == 0
    tile = pl.BlockSpec((tm, LANES), lambda i: (i, 0))
    return pl.pallas_call(
        functools.partial(_router_kernel, tm=tm),
        grid=(t // tm,),
        in_specs=[pl.BlockSpec((tm, d), lambda i: (i, 0)),
                  pl.BlockSpec((d, LANES), lambda i: (0, 0)),
                  pl.BlockSpec((1, LANES), lambda i: (0, 0))],
        out_specs=[tile, tile, tile, pl.BlockSpec((SUBLANES, LANES), lambda i: (0, 0))],
        out_shape=[jax.ShapeDtypeStruct((t, LANES), jnp.int32),
                   jax.ShapeDtypeStruct((t, LANES), F32),
                   jax.ShapeDtypeStruct((t, LANES), jnp.int32),
                   jax.ShapeDtypeStruct((SUBLANES, LANES), F32)],
        scratch_shapes=[pltpu.VMEM((SUBLANES, LANES), F32)],
        compiler_params=_cparams(("arbitrary",)),
        name="router",
    )(h, w_router_p, bias_p)


def _gather_copy(h_hbm, xbuf, sem, tok, slot, r):
    return pltpu.make_async_copy(h_hbm.at[pl.ds(tok, 1), :], xbuf.at[slot, pl.ds(r, 1), :], sem.at[slot])


def _experts_kernel(be_ref, tok_ref, nused_ref, h_hbm, wg_ref, wu_ref, wd_ref, y_ref, xbuf, sem):
    i = pl.program_id(0)
    n_used = nused_ref[0]
    slot = lax.rem(i, 2)

    def issue(blk, s):
        def body(r, carry):
            _gather_copy(h_hbm, xbuf, sem, tok_ref[blk * MOE_ROWS + r], s, r).start()
            return carry
        lax.fori_loop(0, MOE_ROWS, body, 0, unroll=8)

    @pl.when(i == 0)
    def _():
        issue(0, 0)

    @pl.when(i + 1 < n_used)
    def _():
        issue(i + 1, 1 - slot)

    @pl.when(i < n_used)
    def _():
        def wbody(r, carry):
            _gather_copy(h_hbm, xbuf, sem, 0, slot, r).wait()
            return carry
        lax.fori_loop(0, MOE_ROWS, wbody, 0, unroll=8)
        x = xbuf[slot].astype(BF16)
        hid = _silu(_dot(x, wg_ref[...])) * _dot(x, wu_ref[...])
        y_ref[...] = _dot(hid.astype(BF16), wd_ref[...])

    @pl.when(i >= n_used)
    def _():
        y_ref[...] = jnp.zeros(y_ref.shape, y_ref.dtype)


def _experts(h, row_tok, block_e, n_used, wg, wu, wd):
    t, d = h.shape
    n_blocks = block_e.shape[0]
    de = wg.shape[2]

    def blk(i, be, tok, nu):
        return jnp.minimum(i, jnp.maximum(nu[0] - 1, 0))

    return pl.pallas_call(
        _experts_kernel,
        grid_spec=pltpu.PrefetchScalarGridSpec(
            num_scalar_prefetch=3,
            grid=(n_blocks,),
            in_specs=[pl.BlockSpec(memory_space=pl.ANY),
                      pl.BlockSpec((None, d, de), lambda i, be, tok, nu: (be[blk(i, be, tok, nu)], 0, 0)),
                      pl.BlockSpec((None, d, de), lambda i, be, tok, nu: (be[blk(i, be, tok, nu)], 0, 0)),
                      pl.BlockSpec((None, de, d), lambda i, be, tok, nu: (be[blk(i, be, tok, nu)], 0, 0))],
            out_specs=pl.BlockSpec((MOE_ROWS, d), lambda i, be, tok, nu: (i, 0)),
            scratch_shapes=[pltpu.VMEM((2, MOE_ROWS, d), F32), pltpu.SemaphoreType.DMA((2,))]),
        out_shape=jax.ShapeDtypeStruct((n_blocks * MOE_ROWS, d), F32),
        compiler_params=_cparams(("arbitrary",)),
        name="experts",
    )(block_e, row_tok, n_used, h, wg, wu, wd)


def _shared_kernel(h_ref, wg_ref, wu_ref, wd_ref, o_ref):
    x = h_ref[...].astype(BF16)
    hid = _silu(_dot(x, wg_ref[...])) * _dot(x, wu_ref[...])
    o_ref[...] = _dot(hid.astype(BF16), wd_ref[...])


def _shared(h, wg, wu, wd, *, tm=512):
    t, d = h.shape
    ds_ = wg.shape[1]
    return pl.pallas_call(
        _shared_kernel,
        grid=(t // tm,),
        in_specs=[pl.BlockSpec((tm, d), lambda i: (i, 0)),
                  pl.BlockSpec((d, ds_), lambda i: (0, 0)),
                  pl.BlockSpec((d, ds_), lambda i: (0, 0)),
                  pl.BlockSpec((ds_, d), lambda i: (0, 0))],
        out_specs=pl.BlockSpec((tm, d), lambda i: (i, 0)),
        out_shape=jax.ShapeDtypeStruct((t, d), F32),
        compiler_params=_cparams(("parallel",)),
        name="shared_expert",
    )(h, wg, wu, wd)


def _combine_copy(y_hbm, ybuf, sem, row, slot, k, r):
    return pltpu.make_async_copy(y_hbm.at[pl.ds(row, 1), :], ybuf.at[slot, k, pl.ds(r, 1), :], sem.at[slot])


def _combine_kernel(dest_ref, y_hbm, h_ref, sh_ref, gate_ref, g_ref, b_ref, o_ref, ybuf, sem, *, tm):
    i = pl.program_id(0)
    n = pl.num_programs(0)
    slot = lax.rem(i, 2)

    def issue(tile, s):
        def body(r, carry):
            base = (tile * tm + r) * TOP_K
            for k in range(TOP_K):
                _combine_copy(y_hbm, ybuf, sem, dest_ref[base + k], s, k, r).start()
            return carry
        lax.fori_loop(0, tm, body, 0)

    @pl.when(i == 0)
    def _():
        issue(0, 0)

    @pl.when(i + 1 < n)
    def _():
        issue(i + 1, 1 - slot)

    def wbody(r, carry):
        for k in range(TOP_K):
            _combine_copy(y_hbm, ybuf, sem, 0, slot, k, r).wait()
        return carry
    lax.fori_loop(0, tm, wbody, 0)

    gate = gate_ref[...]
    acc = DEEPNORM_ALPHA * h_ref[...] + sh_ref[...]
    for k in range(TOP_K):
        acc = acc + gate[:, k:k + 1] * ybuf[slot, k]
    o_ref[...] = _layer_norm(acc, g_ref[...], b_ref[...])


def _combine(dest_flat, y_rows, h, shared, gate, g, b, *, tm=64):
    t, d = h.shape
    assert t % tm == 0
    return pl.pallas_call(
        functools.partial(_combine_kernel, tm=tm),
        grid_spec=pltpu.PrefetchScalarGridSpec(
            num_scalar_prefetch=1,
            grid=(t // tm,),
            in_specs=[pl.BlockSpec(memory_space=pl.ANY),
                      pl.BlockSpec((tm, d), lambda i, dr: (i, 0)),
                      pl.BlockSpec((tm, d), lambda i, dr: (i, 0)),
                      pl.BlockSpec((tm, LANES), lambda i, dr: (i, 0)),
                      pl.BlockSpec((1, d), lambda i, dr: (0, 0)),
                      pl.BlockSpec((1, d), lambda i, dr: (0, 0))],
            out_specs=pl.BlockSpec((tm, d), lambda i, dr: (i, 0)),
            scratch_shapes=[pltpu.VMEM((2, TOP_K, tm, d), F32), pltpu.SemaphoreType.DMA((2,))]),
        out_shape=jax.ShapeDtypeStruct((t, d), F32),
        compiler_params=_cparams(("arbitrary",)),
        name="combine",
    )(dest_flat, y_rows, h, shared, gate, g, b)


def _layer(x, w_in, conv_w, a_log, dt_bias, norm_g, w_o_swa, w_o_gdn, w_out, ln1_g, ln1_b,
           w_router, router_bias, w_e_gate, w_e_up, w_e_down, w_s_gate, w_s_up, w_s_down, ln2_g, ln2_b):
    batch, seq, d = x.shape
    t = batch * seq
    n_groups = len(SWA_GROUPS)
    swa_w = n_groups * SWA_HEADS_PER_GROUP * HEAD_DIM
    gdn_w = GDN_HEADS * HEAD_DIM
    main_w = 3 * swa_w + 4 * gdn_w
    ba_w = 2 * GDN_HEADS
    assert w_in.shape[1] == main_w + ba_w + 2 * d

    xf = x.reshape(t, d)
    x16 = xf.astype(BF16)

    tn = 512
    assert main_w % tn == 0
    proj_main = _proj(x16, w_in, main_w, tm=1024, tn=tn, out_dtype=BF16, name="in_proj")
    w_ba = jnp.pad(w_in[:, main_w:main_w + ba_w], ((0, 0), (0, LANES - ba_w)))
    ba = _proj(x16, w_ba, LANES, tm=1024, tn=LANES, out_dtype=F32, name="in_proj_ba")
    gates = _proj(x16, w_in[:, main_w + ba_w:], 2 * d, tm=1024, tn=tn, out_dtype=BF16, name="in_proj_gates")

    proj3 = proj_main.reshape(batch, seq, main_w)

    o_groups, lse_groups = [], []
    for j in range(n_groups):
        o, lse = _swa_group(proj3, j, batch, seq, main_w)
        o_groups.append(o)
        lse_groups.append(lse)

    n_chunks = seq // GDN_CHUNK
    ba3 = ba[:, :ba_w].reshape(batch, n_chunks, GDN_CHUNK, ba_w).transpose(0, 3, 1, 2)
    b_t, a_t = ba3[:, :GDN_HEADS], ba3[:, GDN_HEADS:]
    y_b = _gdn(proj3, conv_w.reshape(GDN_CONV, 3 * gdn_w), a_t, b_t, a_log.astype(F32),
               dt_bias.astype(F32), norm_g.reshape(1, HEAD_DIM).astype(F32), batch, seq, 3 * swa_w)
    y_b = y_b.reshape(t, gdn_w)

    merged = _merge(o_groups, lse_groups, y_b, gates, w_o_swa.astype(BF16), w_o_gdn.astype(BF16), d)
    h = _proj_ln(merged, w_out.astype(BF16), xf, ln1_g.reshape(1, d), ln1_b.reshape(1, d))

    e_pad = LANES - N_EXPERTS
    eidx, gate, pos, cnt = _router(h, jnp.pad(w_router, ((0, 0), (0, e_pad))),
                                   jnp.pad(router_bias.reshape(1, N_EXPERTS), ((0, 0), (0, e_pad))))
    counts = cnt[0, :N_EXPERTS].astype(jnp.int32)
    padded = (counts + MOE_ROWS - 1) // MOE_ROWS * MOE_ROWS
    pend = jnp.cumsum(padded)
    pstart = pend - padded
    dest = pstart[eidx[:, :TOP_K]] + pos[:, :TOP_K]
    n_blocks = (t * TOP_K + N_EXPERTS * (MOE_ROWS - 1) + MOE_ROWS - 1) // MOE_ROWS
    dest_flat = dest.reshape(-1)
    tok_of = jnp.arange(t * TOP_K, dtype=jnp.int32) // TOP_K
    row_tok = jnp.zeros((n_blocks * MOE_ROWS,), jnp.int32).at[dest_flat].set(tok_of)
    block_e = jnp.minimum(jnp.searchsorted(pend, jnp.arange(n_blocks, dtype=jnp.int32) * MOE_ROWS, side='right'),
                          N_EXPERTS - 1).astype(jnp.int32)
    n_used = (pend[-1:] // MOE_ROWS).astype(jnp.int32)
    y_rows = _experts(h, row_tok, block_e, n_used, w_e_gate.astype(BF16), w_e_up.astype(BF16),
                      w_e_down.astype(BF16))
    shared = _shared(h, w_s_gate.astype(BF16), w_s_up.astype(BF16), w_s_down.astype(BF16))
    out = _combine(dest_flat, y_rows, h, shared, gate, ln2_g.reshape(1, d), ln2_b.reshape(1, d))
    return out.reshape(batch, seq, d)


@jax.jit
def kernel(x, w_in, conv_w, gdn_a_log, gdn_dt_bias, gdn_norm_g, w_o_swa, w_o_gdn, w_out, ln1_g, ln1_b,
           w_router, router_bias, w_e_gate, w_e_up, w_e_down, w_s_gate, w_s_up, w_s_down, ln2_g, ln2_b):
    for l in range(DEPTH):
        x = _layer(x, w_in[l], conv_w[l], gdn_a_log[l], gdn_dt_bias[l], gdn_norm_g[l], w_o_swa[l],
                   w_o_gdn[l], w_out[l], ln1_g[l], ln1_b[l], w_router[l], router_bias[l], w_e_gate[l],
                   w_e_up[l], w_e_down[l], w_s_gate[l], w_s_up[l], w_s_down[l], ln2_g[l], ln2_b[l])
    return x
```

```python
import functools
import math

import jax
import jax.numpy as jnp
from jax import lax
from jax.experimental import pallas as pl
from jax.experimental.pallas import tpu as pltpu

HEAD_DIM = 128
SWA_GROUPS = ((128, 1), (512, 4), (2048, 16))
SWA_HEADS_PER_GROUP = 8
GDN_HEADS = 16
GDN_CONV = 4
GDN_CHUNK = 64
N_EXPERTS = 96
TOP_K = 8
ROUTED_SCALE = 2.5
LN_EPS = 1e-5
RMS_EPS = 1e-6
DEPTH = 1
DEEPNORM_ALPHA = (2 * DEPTH) ** 0.25

LANES = 128
SUBLANES = 8
VMEM_LIMIT = 56 * 1024 * 1024
MOE_ROWS = 256

F32 = jnp.float32
BF16 = jnp.bfloat16
HIGHEST = lax.Precision.HIGHEST


def _cparams(sem):
    return pltpu.CompilerParams(dimension_semantics=sem, vmem_limit_bytes=VMEM_LIMIT)


def _dot(a, b):
    return jnp.dot(a, b, preferred_element_type=F32)


def _dot_nt(a, b):
    return lax.dot_general(a, b, (((1,), (1,)), ((), ())), preferred_element_type=F32)


def _dot_tn(a, b):
    return lax.dot_general(a, b, (((0,), (0,)), ((), ())), preferred_element_type=F32)


def _dot_hi(a, b):
    return jnp.dot(a, b, preferred_element_type=F32, precision=HIGHEST)


def _sigmoid(x):
    return 1.0 / (1.0 + jnp.exp(-x))


def _silu(x):
    return x * _sigmoid(x)


def _pack_halves(v):
    n = v.shape[1] // 2
    hi = pltpu.bitcast(v[:, :n].astype(BF16).astype(F32), jnp.uint32)
    lo = pltpu.bitcast(v[:, n:].astype(BF16).astype(F32), jnp.uint32)
    return hi | (lo >> 16)


def _unpack_halves(p):
    hi = pltpu.bitcast(p & jnp.uint32(0xFFFF0000), F32)
    lo = pltpu.bitcast(p << 16, F32)
    return hi, lo


def _proj_kernel(a_ref, b_ref, o_ref):
    o_ref[...] = _dot(a_ref[...].astype(BF16), b_ref[...].astype(BF16)).astype(o_ref.dtype)


def _proj(a, b, n_cols, *, tm, tn, out_dtype, col_map=lambda j: j, name="proj"):
    m, k = a.shape
    assert m % tm == 0 and n_cols % tn == 0
    return pl.pallas_call(
        _proj_kernel,
        grid=(m // tm, n_cols // tn),
        in_specs=[pl.BlockSpec((tm, k), lambda i, j: (i, 0)),
                  pl.BlockSpec((k, tn), lambda i, j: (0, col_map(j)))],
        out_specs=pl.BlockSpec((tm, tn), lambda i, j: (i, j)),
        out_shape=jax.ShapeDtypeStruct((m, n_cols), out_dtype),
        compiler_params=_cparams(("parallel", "arbitrary")),
        name=name,
    )(a, b)


def _swa_kernel(q_ref, k_ref, v_ref, o_ref, lse_ref, *, n_heads, span, n_blocks):
    scale = HEAD_DIM ** -0.5
    qi = lax.broadcasted_iota(jnp.int32, (span, 2 * span), 0)
    ci = lax.broadcasted_iota(jnp.int32, (span, 2 * span), 1)
    mask_win = (ci >= qi) & (ci <= qi + span)
    qi0 = lax.broadcasted_iota(jnp.int32, (span, span), 0)
    ci0 = lax.broadcasted_iota(jnp.int32, (span, span), 1)
    mask_first = ci0 <= qi0
    lane = lax.broadcasted_iota(jnp.int32, (span, LANES), 1)

    def attend(q, kw, vw, mask):
        s = _dot_nt(q, kw) * scale
        s = jnp.where(mask, s, -jnp.inf)
        m = jnp.max(s, axis=-1, keepdims=True)
        p = jnp.exp(s - m)
        den = jnp.sum(p, axis=-1, keepdims=True)
        o = _dot(p.astype(BF16), vw) / den
        return o, m + jnp.log(den)

    def block(q0, k0, klen, mask):
        lse_acc = jnp.zeros((span, LANES), F32)
        for h in range(n_heads):
            cs = slice(h * HEAD_DIM, (h + 1) * HEAD_DIM)
            o, lse = attend(q_ref[pl.ds(q0, span), cs], k_ref[pl.ds(k0, klen), cs],
                            v_ref[pl.ds(k0, klen), cs], mask)
            o_ref[pl.ds(q0, span), cs] = o.astype(o_ref.dtype)
            lse_acc = jnp.where(lane == h, lse, lse_acc)
        lse_ref[pl.ds(q0, span), :] = lse_acc

    block(0, 0, span, mask_first)

    def body(n, carry):
        q0 = pl.multiple_of(n * span, span)
        k0 = pl.multiple_of((n - 1) * span, span)
        block(q0, k0, 2 * span, mask_win)
        return carry

    lax.fori_loop(1, n_blocks, body, 0)


def _swa_group(qkv, group, batch, seq):
    window, dil = SWA_GROUPS[group]
    span = window // dil
    sub_len = seq // dil
    assert seq % dil == 0 and sub_len % span == 0 and span % LANES == 0
    gw = SWA_HEADS_PER_GROUP * HEAD_DIM
    view = qkv.reshape(batch, dil, sub_len, 3 * gw)
    kern = functools.partial(_swa_kernel, n_heads=SWA_HEADS_PER_GROUP, span=span,
                             n_blocks=sub_len // span)

    def spec(seg):
        return pl.BlockSpec((None, None, sub_len, gw), lambda b, r: (b, r, 0, seg))

    o, lse = pl.pallas_call(
        kern,
        grid=(batch, dil),
        in_specs=[spec(0), spec(1), spec(2)],
        out_specs=[pl.BlockSpec((None, sub_len, gw), lambda b, r: (b, 0, r)),
                   pl.BlockSpec((None, sub_len, LANES), lambda b, r: (b, 0, r))],
        out_shape=[jax.ShapeDtypeStruct((batch, sub_len, dil * gw), BF16),
                   jax.ShapeDtypeStruct((batch, sub_len, dil * LANES), F32)],
        compiler_params=_cparams(("parallel", "parallel")),
        name=f"swa_g{group}",
    )(view, view, view)
    return o.reshape(batch * seq, gw), lse.reshape(batch * seq, LANES)


def _gdn_kernel(alog_ref, dtb_ref, q_ref, k_ref, v_ref, z_ref, cwq_ref, cwk_ref, cwv_ref,
                a_ref, b_ref, ng_ref, o_ref,
                xpad, qn, kn, vn, gcum_s, qp_s, op_s, m_s, n_s, gl_s, st_s, *, hb, seq):
    C = GDN_CHUNK
    P = 2 * C
    D = HEAD_DIM
    n_chunks = seq // C
    hg = pl.program_id(1)
    pad = SUBLANES

    xpad[0:pad, :] = jnp.zeros((pad, hb * D), F32)
    rows = 256
    for src_ref, cw_ref, dst, mode in ((q_ref, cwq_ref, qn, "q"), (k_ref, cwk_ref, kn, "k"),
                                       (v_ref, cwv_ref, vn, "v")):
        xpad[pad:pad + seq, :] = src_ref[...].astype(F32)
        cw = cw_ref[...]
        for t0 in range(0, seq, rows):
            acc = xpad[pad + t0:pad + t0 + rows, :] * cw[GDN_CONV - 1:GDN_CONV, :]
            for j in range(GDN_CONV - 1):
                sh = GDN_CONV - 1 - j
                acc = acc + xpad[pad + t0 - sh:pad + t0 - sh + rows, :] * cw[j:j + 1, :]
            y = _silu(acc)
            for hh in range(hb):
                cs = slice(hh * D, (hh + 1) * D)
                yh = y[:, cs]
                if mode != "v":
                    yh = yh * lax.rsqrt(jnp.sum(yh * yh, axis=-1, keepdims=True) + 1e-6)
                if mode == "q":
                    yh = yh * (D ** -0.5)
                dst[t0:t0 + rows, cs] = yh

    ri = lax.broadcasted_iota(jnp.int32, (P, P), 0)
    ci = lax.broadcasted_iota(jnp.int32, (P, P), 1)
    same = (ri // C) == (ci // C)
    tri = same & (ri >= ci)
    strict = same & (ri > ci)
    eye = ri == ci
    last = ci == (ri | (C - 1))
    cum_ones = (same & (ri <= ci)).astype(F32)
    eye_f = eye.astype(F32)

    for hh in range(hb):
        head = hg * hb + hh
        sp_in = a_ref[hh] + dtb_ref[head]
        softplus = jnp.maximum(sp_in, 0.0) + jnp.log(1.0 + jnp.exp(-jnp.abs(sp_in)))
        gcum_s[hh] = _dot_hi(-jnp.exp(alog_ref[head]) * softplus, cum_ones)

    heads = range(hb)
    cols = [slice(hh * D, (hh + 1) * D) for hh in heads]

    def pair_body(m, carry):
        r0 = pl.multiple_of(m * P, P)
        g_cum_r = [jnp.broadcast_to(gcum_s[hh, pl.ds(m, 1), :], (P, P)) for hh in heads]
        beta_r = [jnp.broadcast_to(_sigmoid(b_ref[hh, pl.ds(m, 1), :]), (P, P)) for hh in heads]
        g_cum_c = [jnp.sum(jnp.where(eye, g, 0.0), axis=1, keepdims=True) for g in g_cum_r]
        beta_c = [jnp.sum(jnp.where(eye, b, 0.0), axis=1, keepdims=True) for b in beta_r]
        g_last = [jnp.sum(jnp.where(last, g, 0.0), axis=1, keepdims=True) for g in g_cum_r]
        gam = [jnp.exp(jnp.where(tri, g_cum_c[hh] - g_cum_r[hh], -jnp.inf)) for hh in heads]
        q = [qn[pl.ds(r0, P), cs] for cs in cols]
        k = [kn[pl.ds(r0, P), cs] for cs in cols]
        v = [vn[pl.ds(r0, P), cs] for cs in cols]
        kb = [k[hh] * beta_c[hh] for hh in heads]
        k16 = [x.astype(BF16) for x in k]
        pw = [-jnp.where(strict, _dot_nt(kb[hh].astype(BF16), k16[hh]) * gam[hh], 0.0) for hh in heads]
        qk16 = [jnp.where(tri, _dot_nt(q[hh].astype(BF16), k16[hh]) * gam[hh], 0.0).astype(BF16)
                for hh in heads]
        e_g = [jnp.exp(g) for g in g_cum_c]
        rhs16 = [jnp.concatenate([kb[hh] * e_g[hh], v[hh] * beta_c[hh]], axis=1).astype(BF16)
                 for hh in heads]
        kd16 = [(k[hh] * jnp.exp(g_last[hh] - g_cum_c[hh])).astype(BF16) for hh in heads]
        t_inv = [eye_f + x for x in pw]
        for _ in range(int(math.log2(C)) - 1):
            p16 = [x.astype(BF16) for x in pw]
            pw = [_dot(x, x) for x in p16]
            t_inv = [t_inv[hh] + _dot(t_inv[hh].astype(BF16), pw[hh].astype(BF16)) for hh in heads]
        wu16 = [_dot(t_inv[hh].astype(BF16), rhs16[hh]).astype(BF16) for hh in heads]
        qo = [_dot(qk16[hh], wu16[hh]) for hh in heads]
        for hh in heads:
            qp_s[pl.ds(r0, P), cols[hh]] = (q[hh] * e_g[hh] - qo[hh][:, :D]).astype(BF16)
            op_s[pl.ds(r0, P), cols[hh]] = qo[hh][:, D:]
        for c in range(P // C):
            rs = slice(c * C, (c + 1) * C)
            n0 = pl.multiple_of((m * (P // C) + c) * D, D)
            g0 = pl.multiple_of((m * (P // C) + c) * SUBLANES, SUBLANES)
            mn = [_dot_tn(kd16[hh][rs], wu16[hh][rs]) for hh in heads]
            for hh in heads:
                m_s[hh, pl.ds(n0, D), :] = (-mn[hh][:, :D]).astype(BF16)
                n_s[hh, pl.ds(n0, D), :] = mn[hh][:, D:]
                gl_s[hh, pl.ds(g0, SUBLANES), :] = jnp.broadcast_to(
                    jnp.exp(g_last[hh][c * C:c * C + SUBLANES, :]), (SUBLANES, D))
        return carry

    lax.fori_loop(0, seq // P, pair_body, 0, unroll=2)

    st_s[...] = jnp.zeros(st_s.shape, F32)
    ng = ng_ref[...]

    def rec_body(n, carry):
        r0 = pl.multiple_of(n * C, C)
        n0 = pl.multiple_of(n * D, D)
        state = [st_s[hh] for hh in heads]
        s16 = [s.astype(BF16) for s in state]
        upd = [_dot(m_s[hh, pl.ds(n0, D), :], s16[hh]) for hh in heads]
        for hh in heads:
            gl = gl_s[hh, pl.ds(pl.multiple_of(n * SUBLANES, SUBLANES), 1), :]
            st_s[hh] = state[hh] * gl + upd[hh] + n_s[hh, pl.ds(n0, D), :]
        for hh in heads:
            cs = cols[hh]
            o = _dot(qp_s[pl.ds(r0, C), cs], s16[hh]) + op_s[pl.ds(r0, C), cs]
            o = o * lax.rsqrt(jnp.mean(o * o, axis=-1, keepdims=True) + RMS_EPS) * ng
            z = z_ref[pl.ds(r0, C), cs].astype(F32)
            o_ref[pl.ds(r0, C), cs] = (o * _silu(z)).astype(o_ref.dtype)
        return carry

    lax.fori_loop(0, n_chunks, rec_body, 0)


def _gdn(proj3, conv_w2, a_t, b_t, a_log, dt_bias, norm_g, batch, seq, col0, hb=2):
    C = GDN_CHUNK
    H = GDN_HEADS
    D = HEAD_DIM
    P = 2 * C
    bw = hb * D
    assert H % hb == 0 and col0 % bw == 0 and seq % 256 == 0 and P == LANES
    n_hg = H // hb
    c0 = col0 // bw
    nq = H * D // bw
    n_chunks = seq // C
    n_pairs = seq // P

    def pspec(seg):
        return pl.BlockSpec((None, seq, bw), lambda b, g, *_: (b, 0, c0 + seg * nq + g))

    def cwspec(seg):
        return pl.BlockSpec((GDN_CONV, bw), lambda b, g, *_: (0, seg * nq + g))

    abspec = pl.BlockSpec((None, hb, n_pairs, P), lambda b, g, *_: (b, g, 0, 0))
    kern = functools.partial(_gdn_kernel, hb=hb, seq=seq)
    return pl.pallas_call(
        kern,
        grid_spec=pltpu.PrefetchScalarGridSpec(
            num_scalar_prefetch=2,
            grid=(batch, n_hg),
            in_specs=[pspec(0), pspec(1), pspec(2), pspec(3), cwspec(0), cwspec(1), cwspec(2),
                      abspec, abspec, pl.BlockSpec((1, D), lambda b, g, *_: (0, 0))],
            out_specs=pl.BlockSpec((None, seq, bw), lambda b, g, *_: (b, 0, g)),
            scratch_shapes=[
                pltpu.VMEM((seq + SUBLANES, bw), F32),
                pltpu.VMEM((seq, bw), F32),
                pltpu.VMEM((seq, bw), F32),
                pltpu.VMEM((seq, bw), F32),
                pltpu.VMEM((hb, n_pairs, P), F32),
                pltpu.VMEM((seq, bw), BF16),
                pltpu.VMEM((seq, bw), F32),
                pltpu.VMEM((hb, n_chunks * D, D), BF16),
                pltpu.VMEM((hb, n_chunks * D, D), F32),
                pltpu.VMEM((hb, n_chunks * SUBLANES, D), F32),
                pltpu.VMEM((hb, D, D), F32),
            ]),
        out_shape=jax.ShapeDtypeStruct((batch, seq, H * D), BF16),
        compiler_params=_cparams(("parallel", "parallel")),
        name="gdn",
    )(a_log, dt_bias, proj3, proj3, proj3, proj3, conv_w2, conv_w2, conv_w2, a_t, b_t, norm_g)


def _merge_kernel(o0_ref, o1_ref, o2_ref, l0_ref, l1_ref, l2_ref, yb_ref, ga_ref, gb_ref,
                  wa_ref, wb_ref, out_ref, ya_s):
    @pl.when(pl.program_id(1) == 0)
    def _():
        l0, l1, l2 = l0_ref[...], l1_ref[...], l2_ref[...]
        m = jnp.maximum(jnp.maximum(l0, l1), l2)
        e0, e1, e2 = jnp.exp(l0 - m), jnp.exp(l1 - m), jnp.exp(l2 - m)
        inv = 1.0 / (e0 + e1 + e2)
        w0, w1, w2 = e0 * inv, e1 * inv, e2 * inv
        for h in range(SWA_HEADS_PER_GROUP):
            cs = slice(h * HEAD_DIM, (h + 1) * HEAD_DIM)
            ya = (w0[:, h:h + 1] * o0_ref[:, cs].astype(F32) + w1[:, h:h + 1] * o1_ref[:, cs].astype(F32)
                  + w2[:, h:h + 1] * o2_ref[:, cs].astype(F32))
            ya_s[:, cs] = ya.astype(BF16)

    pa = _dot(ya_s[...], wa_ref[...])
    pb = _dot(yb_ref[...], wb_ref[...])
    out_ref[...] = (_sigmoid(ga_ref[...].astype(F32)) * pa
                    + _sigmoid(gb_ref[...].astype(F32)) * pb).astype(out_ref.dtype)


def _merge(o_groups, lse_groups, y_b, gates, w_o_swa, w_o_gdn, d_model, *, tm=512, tn=512):
    t = y_b.shape[0]
    wa_k, wb_k = w_o_swa.shape[0], w_o_gdn.shape[0]
    assert t % tm == 0 and d_model % tn == 0
    nb = d_model // tn
    row = lambda w: pl.BlockSpec((tm, w), lambda i, j: (i, 0))
    return pl.pallas_call(
        _merge_kernel,
        grid=(t // tm, nb),
        in_specs=[row(wa_k), row(wa_k), row(wa_k), row(LANES), row(LANES), row(LANES), row(wb_k),
                  pl.BlockSpec((tm, tn), lambda i, j: (i, j)),
                  pl.BlockSpec((tm, tn), lambda i, j: (i, nb + j)),
                  pl.BlockSpec((wa_k, tn), lambda i, j: (0, j)),
                  pl.BlockSpec((wb_k, tn), lambda i, j: (0, j))],
        out_specs=pl.BlockSpec((tm, tn), lambda i, j: (i, j)),
        out_shape=jax.ShapeDtypeStruct((t, d_model), BF16),
        scratch_shapes=[pltpu.VMEM((tm, wa_k), BF16)],
        compiler_params=_cparams(("parallel", "arbitrary")),
        name="merge",
    )(*o_groups, *lse_groups, y_b, gates, gates, w_o_swa, w_o_gdn)


def _layer_norm(v, g, b):
    mu = jnp.mean(v, axis=-1, keepdims=True)
    c = v - mu
    var = jnp.mean(c * c, axis=-1, keepdims=True)
    return c * lax.rsqrt(var + LN_EPS) * g + b


def _proj_ln_kernel(a_ref, w_ref, res_ref, g_ref, b_ref, o_ref, op_ref, *, tm):
    k = pl.program_id(1)
    part = _dot(a_ref[...], w_ref[...])

    @pl.when(k == 0)
    def _():
        o_ref[...] = part

    @pl.when(k > 0)
    def _():
        o_ref[...] += part

    @pl.when(k == pl.num_programs(1) - 1)
    def _():
        slab = 64
        for r in range(0, tm, slab):
            rs = slice(r, r + slab)
            hn = _layer_norm(DEEPNORM_ALPHA * res_ref[rs, :] + o_ref[rs, :], g_ref[...], b_ref[...])
            o_ref[rs, :] = hn
            op_ref[rs, :] = _pack_halves(hn)


def _proj_ln(a, w, res, g, b, *, tm=512, tk=256):
    t, kdim = a.shape
    d = w.shape[1]
    assert t % tm == 0 and kdim % tk == 0
    return pl.pallas_call(
        functools.partial(_proj_ln_kernel, tm=tm),
        grid=(t // tm, kdim // tk),
        in_specs=[pl.BlockSpec((tm, tk), lambda i, k: (i, k)),
                  pl.BlockSpec((tk, d), lambda i, k: (k, 0)),
                  pl.BlockSpec((tm, d), lambda i, k: (i, 0), pipeline_mode=pl.Buffered(1)),
                  pl.BlockSpec((1, d), lambda i, k: (0, 0)),
                  pl.BlockSpec((1, d), lambda i, k: (0, 0))],
        out_specs=[pl.BlockSpec((tm, d), lambda i, k: (i, 0)),
                   pl.BlockSpec((tm, d // 2), lambda i, k: (i, 0))],
        out_shape=[jax.ShapeDtypeStruct((t, d), F32),
                   jax.ShapeDtypeStruct((t, d // 2), jnp.uint32)],
        compiler_params=_cparams(("parallel", "arbitrary")),
        name="proj_ln",
    )(a, w, res, g, b)


def _router_kernel(h_ref, w_ref, bias_ref, eidx_ref, gate_ref, pos_ref, cnt_ref, cnt_s, *, tm):
    i = pl.program_id(0)

    @pl.when(i == 0)
    def _():
        cnt_s[...] = jnp.zeros(cnt_s.shape, F32)

    scores = _sigmoid(_dot_hi(h_ref[...], w_ref[...]))
    lane = lax.broadcasted_iota(jnp.int32, (tm, LANES), 1)
    sel = jnp.where(lane < N_EXPERTS, scores + bias_ref[...], -jnp.inf)
    chosen = jnp.zeros((tm, LANES), jnp.bool_)
    eidx = jnp.zeros((tm, LANES), jnp.int32)
    gsel = jnp.zeros((tm, LANES), F32)
    picks = []
    for k in range(TOP_K):
        m = jnp.max(sel, axis=-1, keepdims=True)
        idx = jnp.min(jnp.where(sel == m, lane, LANES), axis=-1, keepdims=True)
        picks.append(idx)
        hit = lane == idx
        sc = jnp.sum(jnp.where(hit, scores, 0.0), axis=-1, keepdims=True)
        eidx = jnp.where(lane == k, idx, eidx)
        gsel = jnp.where(lane == k, sc, gsel)
        chosen = chosen | hit
        sel = jnp.where(hit, -jnp.inf, sel)
    gate_ref[...] = gsel / jnp.sum(gsel, axis=-1, keepdims=True) * ROUTED_SCALE

    ri = lax.broadcasted_iota(jnp.int32, (tm, tm), 0)
    ci = lax.broadcasted_iota(jnp.int32, (tm, tm), 1)
    ch = jnp.where(chosen, 1.0, 0.0)
    rank = _dot(jnp.where(ri > ci, 1.0, 0.0).astype(BF16), ch.astype(BF16)) + cnt_s[0:1, :]
    pos = jnp.zeros((tm, LANES), F32)
    for k in range(TOP_K):
        pk = jnp.sum(jnp.where(lane == picks[k], rank, 0.0), axis=-1, keepdims=True)
        pos = jnp.where(lane == k, pk, pos)
    pos_ref[...] = pos.astype(jnp.int32)
    eidx_ref[...] = eidx
    total = cnt_s[...] + jnp.sum(ch, axis=0, keepdims=True)
    cnt_s[...] = total
    cnt_ref[...] = total


def _router(h, w_router_p, bias_p, *, tm=256):
    t, d = h.shape
    assert t % tm == 0
    tile = pl.BlockSpec((tm, LANES), lambda i: (i, 0))
    return pl.pallas_call(
        functools.partial(_router_kernel, tm=tm),
        grid=(t // tm,),
        in_specs=[pl.BlockSpec((tm, d), lambda i: (i, 0)),
                  pl.BlockSpec((d, LANES), lambda i: (0, 0)),
                  pl.BlockSpec((1, LANES), lambda i: (0, 0))],
        out_specs=[tile, tile, tile, pl.BlockSpec((SUBLANES, LANES), lambda i: (0, 0))],
        out_shape=[jax.ShapeDtypeStruct((t, LANES), jnp.int32),
                   jax.ShapeDtypeStruct((t, LANES), F32),
                   jax.ShapeDtypeStruct((t, LANES), jnp.int32),
                   jax.ShapeDtypeStruct((SUBLANES, LANES), F32)],
        scratch_shapes=[pltpu.VMEM((SUBLANES, LANES), F32)],
        compiler_params=_cparams(("arbitrary",)),
        name="router",
    )(h, w_router_p, bias_p)


def _gather_copy(h_hbm, xbuf, sem, tok, slot, r):
    return pltpu.make_async_copy(h_hbm.at[pl.ds(tok, 1), :], xbuf.at[slot, pl.ds(r, 1), :], sem.at[slot])


def _experts_kernel(be_ref, tok_ref, nused_ref, h_hbm, wg_ref, wu_ref, wd_ref, y_ref, xbuf, sem):
    i = pl.program_id(0)
    n_used = nused_ref[0]
    slot = lax.rem(i, 2)

    def wait_slot(s):
        def wbody(r, carry):
            _gather_copy(h_hbm, xbuf, sem, 0, s, r).wait()
            return carry
        lax.fori_loop(0, MOE_ROWS, wbody, 0, unroll=8)

    @pl.when(i == 0)
    def _():
        def body(r, carry):
            _gather_copy(h_hbm, xbuf, sem, tok_ref[r], 0, r).start()
            return carry
        lax.fori_loop(0, MOE_ROWS, body, 0, unroll=8)

    @pl.when(i < n_used)
    def _():
        wait_slot(slot)
        hi, lo = _unpack_halves(xbuf[slot])
        x = jnp.concatenate([hi.astype(BF16), lo.astype(BF16)], axis=1)
        base = jnp.minimum(i + 1, n_used - 1) * MOE_ROWS
        for r in range(MOE_ROWS):
            _gather_copy(h_hbm, xbuf, sem, tok_ref[base + r], 1 - slot, r).start()
        hid = _silu(_dot(x, wg_ref[...])) * _dot(x, wu_ref[...])
        y_ref[...] = _pack_halves(_dot(hid.astype(BF16), wd_ref[...]))

    @pl.when(i == n_used - 1)
    def _():
        wait_slot(1 - slot)

    @pl.when(i >= n_used)
    def _():
        y_ref[...] = jnp.zeros(y_ref.shape, y_ref.dtype)


def _experts(h, row_tok, block_e, n_used, wg, wu, wd):
    t, dh = h.shape
    d = 2 * dh
    n_blocks = block_e.shape[0]
    de = wg.shape[2]

    def blk(i, be, tok, nu):
        return jnp.minimum(i, jnp.maximum(nu[0] - 1, 0))

    return pl.pallas_call(
        _experts_kernel,
        grid_spec=pltpu.PrefetchScalarGridSpec(
            num_scalar_prefetch=3,
            grid=(n_blocks,),
            in_specs=[pl.BlockSpec(memory_space=pl.ANY),
                      pl.BlockSpec((None, d, de), lambda i, be, tok, nu: (be[blk(i, be, tok, nu)], 0, 0)),
                      pl.BlockSpec((None, d, de), lambda i, be, tok, nu: (be[blk(i, be, tok, nu)], 0, 0)),
                      pl.BlockSpec((None, de, d), lambda i, be, tok, nu: (be[blk(i, be, tok, nu)], 0, 0))],
            out_specs=pl.BlockSpec((MOE_ROWS, dh), lambda i, be, tok, nu: (i, 0)),
            scratch_shapes=[pltpu.VMEM((2, MOE_ROWS, dh), jnp.uint32), pltpu.SemaphoreType.DMA((2,))]),
        out_shape=jax.ShapeDtypeStruct((n_blocks * MOE_ROWS, dh), jnp.uint32),
        compiler_params=_cparams(("arbitrary",)),
        name="experts",
    )(block_e, row_tok, n_used, h, wg, wu, wd)


def _shared_kernel(h_ref, wg_ref, wu_ref, wd_ref, o_ref):
    x = h_ref[...].astype(BF16)
    hid = _silu(_dot(x, wg_ref[...])) * _dot(x, wu_ref[...])
    o_ref[...] = _dot(hid.astype(BF16), wd_ref[...])


def _shared(h, wg, wu, wd, *, tm=512):
    t, d = h.shape
    ds_ = wg.shape[1]
    return pl.pallas_call(
        _shared_kernel,
        grid=(t // tm,),
        in_specs=[pl.BlockSpec((tm, d), lambda i: (i, 0)),
                  pl.BlockSpec((d, ds_), lambda i: (0, 0)),
                  pl.BlockSpec((d, ds_), lambda i: (0, 0)),
                  pl.BlockSpec((ds_, d), lambda i: (0, 0))],
        out_specs=pl.BlockSpec((tm, d), lambda i: (i, 0)),
        out_shape=jax.ShapeDtypeStruct((t, d), F32),
        compiler_params=_cparams(("parallel",)),
        name="shared_expert",
    )(h, wg, wu, wd)


def _combine_copy(y_hbm, ybuf, sem, row, slot, k, r):
    return pltpu.make_async_copy(y_hbm.at[pl.ds(row, 1), :], ybuf.at[slot, k, pl.ds(r, 1), :], sem.at[slot])


def _combine_kernel(dest_ref, y_hbm, h_ref, sh_ref, gate_ref, g_ref, b_ref, o_ref, ybuf, sem, *, tm):
    i = pl.program_id(0)
    n = pl.num_programs(0)
    slot = lax.rem(i, 2)

    def issue(tile, s):
        def body(r, carry):
            base = (tile * tm + r) * TOP_K
            for k in range(TOP_K):
                _combine_copy(y_hbm, ybuf, sem, dest_ref[base + k], s, k, r).start()
            return carry
        lax.fori_loop(0, tm, body, 0)

    @pl.when(i == 0)
    def _():
        issue(0, 0)

    @pl.when(i + 1 < n)
    def _():
        issue(i + 1, 1 - slot)

    def wbody(r, carry):
        for k in range(TOP_K):
            _combine_copy(y_hbm, ybuf, sem, 0, slot, k, r).wait()
        return carry
    lax.fori_loop(0, tm, wbody, 0)

    slab = 32
    dh = h_ref.shape[1] // 2
    for r in range(0, tm, slab):
        rs = slice(r, r + slab)
        gate = gate_ref[rs, :]
        res = DEEPNORM_ALPHA * h_ref[rs, :] + sh_ref[rs, :]
        acc_hi, acc_lo = res[:, :dh], res[:, dh:]
        for k in range(TOP_K):
            hi, lo = _unpack_halves(ybuf[slot, k, rs, :])
            acc_hi = acc_hi + gate[:, k:k + 1] * hi
            acc_lo = acc_lo + gate[:, k:k + 1] * lo
        o_ref[rs, :] = _layer_norm(jnp.concatenate([acc_hi, acc_lo], axis=1), g_ref[...], b_ref[...])


def _combine(dest_flat, y_rows, h, shared, gate, g, b, *, tm=128):
    t, d = h.shape
    assert t % tm == 0
    return pl.pallas_call(
        functools.partial(_combine_kernel, tm=tm),
        grid_spec=pltpu.PrefetchScalarGridSpec(
            num_scalar_prefetch=1,
            grid=(t // tm,),
            in_specs=[pl.BlockSpec(memory_space=pl.ANY),
                      pl.BlockSpec((tm, d), lambda i, dr: (i, 0)),
                      pl.BlockSpec((tm, d), lambda i, dr: (i, 0)),
                      pl.BlockSpec((tm, LANES), lambda i, dr: (i, 0)),
                      pl.BlockSpec((1, d), lambda i, dr: (0, 0)),
                      pl.BlockSpec((1, d), lambda i, dr: (0, 0))],
            out_specs=pl.BlockSpec((tm, d), lambda i, dr: (i, 0)),
            scratch_shapes=[pltpu.VMEM((2, TOP_K, tm, d // 2), jnp.uint32), pltpu.SemaphoreType.DMA((2,))]),
        out_shape=jax.ShapeDtypeStruct((t, d), F32),
        compiler_params=_cparams(("arbitrary",)),
        name="combine",
    )(dest_flat, y_rows, h, shared, gate, g, b)


def _layer(x, w_in, conv_w, a_log, dt_bias, norm_g, w_o_swa, w_o_gdn, w_out, ln1_g, ln1_b,
           w_router, router_bias, w_e_gate, w_e_up, w_e_down, w_s_gate, w_s_up, w_s_down, ln2_g, ln2_b):
    batch, seq, d = x.shape
    t = batch * seq
    n_groups = len(SWA_GROUPS)
    swa_w = n_groups * SWA_HEADS_PER_GROUP * HEAD_DIM
    gdn_w = GDN_HEADS * HEAD_DIM
    main_w = 3 * swa_w + 4 * gdn_w
    ba_w = 2 * GDN_HEADS
    assert w_in.shape[1] == main_w + ba_w + 2 * d

    xf = x.reshape(t, d)
    x16 = xf.astype(BF16)

    tn = 512
    gw = SWA_HEADS_PER_GROUP * HEAD_DIM
    assert gw % tn == 0 and (3 * swa_w) % tn == 0 and (4 * gdn_w) % tn == 0
    bpg = gw // tn

    o_groups, lse_groups = [], []
    for j, (_, dil) in enumerate(SWA_GROUPS):
        xj = x16 if dil == 1 else (
            x.reshape(batch, seq // dil, dil, d).transpose(0, 2, 1, 3).reshape(t, d).astype(BF16))
        qkv = _proj(xj, w_in, 3 * gw, tm=1024, tn=tn, out_dtype=BF16, name=f"in_proj_swa{j}",
                    col_map=lambda c, j=j: ((c // bpg) * n_groups + j) * bpg + c % bpg)
        o, lse = _swa_group(qkv, j, batch, seq)
        o_groups.append(o)
        lse_groups.append(lse)

    gdn_in = _proj(x16, w_in, 4 * gdn_w, tm=1024, tn=tn, out_dtype=BF16, name="in_proj_gdn",
                   col_map=lambda c: c + 3 * swa_w // tn)
    w_ba = jnp.pad(w_in[:, main_w:main_w + ba_w], ((0, 0), (0, LANES - ba_w)))
    ba = _proj(x16, w_ba, LANES, tm=1024, tn=LANES, out_dtype=F32, name="in_proj_ba")
    gates = _proj(x16, w_in[:, main_w + ba_w:], 2 * d, tm=1024, tn=tn, out_dtype=BF16, name="in_proj_gates")
    pair = 2 * GDN_CHUNK
    ba3 = ba[:, :ba_w].reshape(batch, seq // pair, pair, ba_w).transpose(0, 3, 1, 2)
    b_t, a_t = ba3[:, :GDN_HEADS], ba3[:, GDN_HEADS:]
    y_b = _gdn(gdn_in.reshape(batch, seq, 4 * gdn_w), conv_w.reshape(GDN_CONV, 3 * gdn_w), a_t, b_t,
               a_log.astype(F32), dt_bias.astype(F32), norm_g.reshape(1, HEAD_DIM).astype(F32), batch, seq, 0)
    y_b = y_b.reshape(t, gdn_w)

    merged = _merge(o_groups, lse_groups, y_b, gates, w_o_swa.astype(BF16), w_o_gdn.astype(BF16), d)
    h, h_packed = _proj_ln(merged, w_out.astype(BF16), xf, ln1_g.reshape(1, d), ln1_b.reshape(1, d))

    e_pad = LANES - N_EXPERTS
    eidx, gate, pos, cnt = _router(h, jnp.pad(w_router, ((0, 0), (0, e_pad))),
                                   jnp.pad(router_bias.reshape(1, N_EXPERTS), ((0, 0), (0, e_pad))))
    counts = cnt[0, :N_EXPERTS].astype(jnp.int32)
    padded = (counts + MOE_ROWS - 1) // MOE_ROWS * MOE_ROWS
    pend = jnp.cumsum(padded)
    pstart = pend - padded
    e_sel = eidx[:, :TOP_K, None] == jnp.arange(N_EXPERTS, dtype=jnp.int32)
    dest = jnp.sum(jnp.where(e_sel, pstart, 0), axis=-1) + pos[:, :TOP_K]
    n_blocks = (t * TOP_K + N_EXPERTS * (MOE_ROWS - 1) + MOE_ROWS - 1) // MOE_ROWS
    dest_flat = dest.reshape(-1)
    tok_of = jnp.arange(t * TOP_K, dtype=jnp.int32) // TOP_K
    row_tok = jnp.zeros((n_blocks * MOE_ROWS,), jnp.int32).at[dest_flat].set(tok_of)
    block_e = jnp.minimum(jnp.searchsorted(pend, jnp.arange(n_blocks, dtype=jnp.int32) * MOE_ROWS, side='right'),
                          N_EXPERTS - 1).astype(jnp.int32)
    n_used = (pend[-1:] // MOE_ROWS).astype(jnp.int32)
    y_rows = _experts(h_packed, row_tok, block_e, n_used, w_e_gate.astype(BF16), w_e_up.astype(BF16),
                      w_e_down.astype(BF16))
    shared = _shared(h, w_s_gate.astype(BF16), w_s_up.astype(BF16), w_s_down.astype(BF16))
    out = _combine(dest_flat, y_rows, h, shared, gate, ln2_g.reshape(1, d), ln2_b.reshape(1, d))
    return out.reshape(batch, seq, d)


@jax.jit
def kernel(x, w_in, conv_w, gdn_a_log, gdn_dt_bias, gdn_norm_g, w_o_swa, w_o_gdn, w_out, ln1_g, ln1_b,
           w_router, router_bias, w_e_gate, w_e_up, w_e_down, w_s_gate, w_s_up, w_s_down, ln2_g, ln2_b):
    for l in range(DEPTH):
        x = _layer(x, w_in[l], conv_w[l], gdn_a_log[l], gdn_dt_bias[l], gdn_norm_g[l], w_o_swa[l],
                   w_o_gdn[l], w_out[l], ln1_g[l], ln1_b[l], w_router[l], router_bias[l], w_e_gate[l],
                   w_e_up[l], w_e_down[l], w_s_gate[l], w_s_up[l], w_s_down[l], ln2_g[l], ln2_b[l])
    return x
```

```python
import functools
import math

import jax
import jax.numpy as jnp
from jax import lax
from jax.experimental import pallas as pl
from jax.experimental.pallas import tpu as pltpu

HEAD_DIM = 128
SWA_GROUPS = ((128, 1), (512, 4), (2048, 16))
SWA_HEADS_PER_GROUP = 8
GDN_HEADS = 16
GDN_CONV = 4
GDN_CHUNK = 64
N_EXPERTS = 96
TOP_K = 8
ROUTED_SCALE = 2.5
LN_EPS = 1e-5
RMS_EPS = 1e-6
DEPTH = 1
DEEPNORM_ALPHA = (2 * DEPTH) ** 0.25

LANES = 128
SUBLANES = 8
VMEM_LIMIT = 58 * 1024 * 1024
MOE_ROWS = 256

F32 = jnp.float32
BF16 = jnp.bfloat16
HIGHEST = lax.Precision.HIGHEST


def _cparams(sem):
    return pltpu.CompilerParams(dimension_semantics=sem, vmem_limit_bytes=VMEM_LIMIT)


def _dot(a, b):
    return jnp.dot(a, b, preferred_element_type=F32)


def _dot_nt(a, b):
    return lax.dot_general(a, b, (((1,), (1,)), ((), ())), preferred_element_type=F32)


def _dot_tn(a, b):
    return lax.dot_general(a, b, (((0,), (0,)), ((), ())), preferred_element_type=F32)


def _dot_hi(a, b):
    return jnp.dot(a, b, preferred_element_type=F32, precision=HIGHEST)


def _sigmoid(x):
    return 1.0 / (1.0 + jnp.exp(-x))


def _silu(x):
    return x * _sigmoid(x)


def _pack_halves(v):
    n = v.shape[1] // 2
    hi = pltpu.bitcast(v[:, :n].astype(BF16).astype(F32), jnp.uint32)
    lo = pltpu.bitcast(v[:, n:].astype(BF16).astype(F32), jnp.uint32)
    return hi | (lo >> 16)


def _unpack_halves(p):
    hi = pltpu.bitcast(p & jnp.uint32(0xFFFF0000), F32)
    lo = pltpu.bitcast(p << 16, F32)
    return hi, lo


def _proj_kernel(a_ref, bt_ref, o_ref):
    o_ref[...] = _dot_nt(a_ref[...].astype(BF16), bt_ref[...].astype(BF16)).astype(o_ref.dtype)


def _proj(a, bt, n_cols, *, tm, tn, out_dtype, col_map=lambda j: j, name="proj"):
    m, k = a.shape
    assert m % tm == 0 and n_cols % tn == 0 and bt.shape[1] == k
    return pl.pallas_call(
        _proj_kernel,
        grid=(m // tm, n_cols // tn),
        in_specs=[pl.BlockSpec((tm, k), lambda i, j: (i, 0)),
                  pl.BlockSpec((tn, k), lambda i, j: (col_map(j), 0))],
        out_specs=pl.BlockSpec((tm, tn), lambda i, j: (i, j)),
        out_shape=jax.ShapeDtypeStruct((m, n_cols), out_dtype),
        compiler_params=_cparams(("parallel", "arbitrary")),
        name=name,
    )(a, bt)


def _proj_residue_kernel(a_ref, bt_ref, o_ref, y_s, *, dil):
    y = _dot_nt(a_ref[...].astype(BF16), bt_ref[...].astype(BF16))
    rows = y_s.shape[1] // dil
    for c in range(y_s.shape[0]):
        cs = slice(c * LANES, (c + 1) * LANES)
        y_s[c] = y[:, cs]
        for r in range(dil):
            o_ref[r, :, cs] = y_s[c, pl.ds(r, rows, stride=dil), :].astype(o_ref.dtype)


def _proj_residue(a, bt, n_cols, batch, seq, dil, *, tm, tn, out_dtype, col_map, name):
    m, k = a.shape
    assert seq % tm == 0 and tm % (dil * 16) == 0 and n_cols % tn == 0 and bt.shape[1] == k
    spb = seq // tm
    out = pl.pallas_call(
        functools.partial(_proj_residue_kernel, dil=dil),
        grid=(m // tm, n_cols // tn),
        in_specs=[pl.BlockSpec((tm, k), lambda i, j: (i, 0)),
                  pl.BlockSpec((tn, k), lambda i, j: (col_map(j), 0))],
        out_specs=pl.BlockSpec((None, dil, tm // dil, tn), lambda i, j: (i // spb, 0, i % spb, j)),
        out_shape=jax.ShapeDtypeStruct((batch, dil, seq // dil, n_cols), out_dtype),
        scratch_shapes=[pltpu.VMEM((tn // LANES, tm, LANES), F32)],
        compiler_params=_cparams(("parallel", "arbitrary")),
        name=name,
    )(a, bt)
    return out.reshape(m, n_cols)


def _swa_kernel(q_ref, k_ref, v_ref, o_ref, lse_ref, *, n_heads, span, n_blocks):
    scale = HEAD_DIM ** -0.5
    qi = lax.broadcasted_iota(jnp.int32, (span, 2 * span), 0)
    ci = lax.broadcasted_iota(jnp.int32, (span, 2 * span), 1)
    mask_win = (ci >= qi) & (ci <= qi + span)
    qi0 = lax.broadcasted_iota(jnp.int32, (span, span), 0)
    ci0 = lax.broadcasted_iota(jnp.int32, (span, span), 1)
    mask_first = ci0 <= qi0
    lane = lax.broadcasted_iota(jnp.int32, (span, LANES), 1)

    def attend(q, kw, vw, mask):
        s = _dot_nt(q, kw) * scale
        s = jnp.where(mask, s, -jnp.inf)
        m = jnp.max(s, axis=-1, keepdims=True)
        p = jnp.exp(s - m)
        den = jnp.sum(p, axis=-1, keepdims=True)
        o = _dot(p.astype(BF16), vw) / den
        return o, m + jnp.log(den)

    def block(q0, k0, klen, mask):
        lse_acc = jnp.zeros((span, LANES), F32)
        for h in range(n_heads):
            cs = slice(h * HEAD_DIM, (h + 1) * HEAD_DIM)
            o, lse = attend(q_ref[pl.ds(q0, span), cs], k_ref[pl.ds(k0, klen), cs],
                            v_ref[pl.ds(k0, klen), cs], mask)
            o_ref[pl.ds(q0, span), cs] = o.astype(o_ref.dtype)
            lse_acc = jnp.where(lane == h, lse, lse_acc)
        lse_ref[pl.ds(q0, span), :] = lse_acc

    block(0, 0, span, mask_first)

    def body(n, carry):
        q0 = pl.multiple_of(n * span, span)
        k0 = pl.multiple_of((n - 1) * span, span)
        block(q0, k0, 2 * span, mask_win)
        return carry

    lax.fori_loop(1, n_blocks, body, 0)


def _swa_group(qkv, group, batch, seq):
    window, dil = SWA_GROUPS[group]
    span = window // dil
    sub_len = seq // dil
    assert seq % dil == 0 and sub_len % span == 0 and span % LANES == 0
    gw = SWA_HEADS_PER_GROUP * HEAD_DIM
    view = qkv.reshape(batch, dil, sub_len, 3 * gw)
    kern = functools.partial(_swa_kernel, n_heads=SWA_HEADS_PER_GROUP, span=span,
                             n_blocks=sub_len // span)

    def spec(seg):
        return pl.BlockSpec((None, None, sub_len, gw), lambda b, r: (b, r, 0, seg))

    o, lse = pl.pallas_call(
        kern,
        grid=(batch, dil),
        in_specs=[spec(0), spec(1), spec(2)],
        out_specs=[pl.BlockSpec((None, sub_len, gw), lambda b, r: (b, 0, r)),
                   pl.BlockSpec((None, sub_len, LANES), lambda b, r: (b, 0, r))],
        out_shape=[jax.ShapeDtypeStruct((batch, sub_len, dil * gw), BF16),
                   jax.ShapeDtypeStruct((batch, sub_len, dil * LANES), F32)],
        compiler_params=_cparams(("parallel", "parallel")),
        name=f"swa_g{group}",
    )(view, view, view)
    return o.reshape(batch * seq, gw), lse.reshape(batch * seq, LANES)


def _gdn_kernel(alog_ref, dtb_ref, q_ref, k_ref, v_ref, z_ref, cwq_ref, cwk_ref, cwv_ref,
                a_ref, b_ref, ng_ref, o_ref,
                xpad, qn, kn, vn, gcum_s, mq_s, op_s, n_s, gl_s, *, hb, seq):
    C = GDN_CHUNK
    P = 2 * C
    D = HEAD_DIM
    n_chunks = seq // C
    hg = pl.program_id(1)
    pad = SUBLANES

    xpad[0:pad, :] = jnp.zeros((pad, hb * D), F32)
    rows = 256
    for src_ref, cw_ref, dst, mode in ((q_ref, cwq_ref, qn, "q"), (k_ref, cwk_ref, kn, "k"),
                                       (v_ref, cwv_ref, vn, "v")):
        xpad[pad:pad + seq, :] = src_ref[...].astype(F32)
        cw = cw_ref[...]
        for t0 in range(0, seq, rows):
            acc = xpad[pad + t0:pad + t0 + rows, :] * cw[GDN_CONV - 1:GDN_CONV, :]
            for j in range(GDN_CONV - 1):
                sh = GDN_CONV - 1 - j
                acc = acc + xpad[pad + t0 - sh:pad + t0 - sh + rows, :] * cw[j:j + 1, :]
            y = _silu(acc)
            for hh in range(hb):
                cs = slice(hh * D, (hh + 1) * D)
                yh = y[:, cs]
                if mode != "v":
                    yh = yh * lax.rsqrt(jnp.sum(yh * yh, axis=-1, keepdims=True) + 1e-6)
                if mode == "q":
                    yh = yh * (D ** -0.5)
                dst[t0:t0 + rows, cs] = yh

    ri = lax.broadcasted_iota(jnp.int32, (P, P), 0)
    ci = lax.broadcasted_iota(jnp.int32, (P, P), 1)
    same = (ri // C) == (ci // C)
    tri = same & (ri >= ci)
    strict = same & (ri > ci)
    eye = ri == ci
    last = ci == (ri | (C - 1))
    cum_ones = (same & (ri <= ci)).astype(F32)
    eye_f = eye.astype(F32)
    first_chunk = lax.broadcasted_iota(jnp.int32, (P, D), 0) < C

    for hh in range(hb):
        head = hg * hb + hh
        sp_in = a_ref[hh] + dtb_ref[head]
        softplus = jnp.maximum(sp_in, 0.0) + jnp.log(1.0 + jnp.exp(-jnp.abs(sp_in)))
        gcum_s[hh] = _dot_hi(-jnp.exp(alog_ref[head]) * softplus, cum_ones)

    heads = range(hb)
    cols = [slice(hh * D, (hh + 1) * D) for hh in heads]

    pairs_per_step = 2
    items = [(g, hh) for g in range(pairs_per_step) for hh in heads]
    its = range(len(items))

    def pair_body(mb, carry):
        mp = [mb * pairs_per_step + g for g, _ in items]
        r0 = [pl.multiple_of(m * P, P) for m in mp]
        hd = [hh for _, hh in items]
        g_cum_r = [jnp.broadcast_to(gcum_s[hd[i], pl.ds(mp[i], 1), :], (P, P)) for i in its]
        beta_r = [jnp.broadcast_to(_sigmoid(b_ref[hd[i], pl.ds(mp[i], 1), :]), (P, P)) for i in its]
        q = [qn[pl.ds(r0[i], P), cols[hd[i]]] for i in its]
        k = [kn[pl.ds(r0[i], P), cols[hd[i]]] for i in its]
        v = [vn[pl.ds(r0[i], P), cols[hd[i]]] for i in its]
        g_cum_c = [jnp.sum(jnp.where(eye, g, 0.0), axis=1, keepdims=True) for g in g_cum_r]
        beta_c = [jnp.sum(jnp.where(eye, b, 0.0), axis=1, keepdims=True) for b in beta_r]
        g_last = [jnp.sum(jnp.where(last, g, 0.0), axis=1, keepdims=True) for g in g_cum_r]
        gam = [jnp.exp(jnp.where(tri, g_cum_c[i] - g_cum_r[i], -jnp.inf)) for i in its]
        kb = [k[i] * beta_c[i] for i in its]
        k16 = [x.astype(BF16) for x in k]
        aq = [_dot_nt(jnp.concatenate([kb[i].astype(BF16), q[i].astype(BF16)], axis=0), k16[i])
              for i in its]
        pw = [-jnp.where(strict, aq[i][:P] * gam[i], 0.0) for i in its]
        qk16 = [jnp.where(tri, aq[i][P:] * gam[i], 0.0).astype(BF16) for i in its]
        e_g = [jnp.exp(g) for g in g_cum_c]
        rhs16 = [jnp.concatenate([kb[i] * e_g[i], v[i] * beta_c[i]], axis=1).astype(BF16)
                 for i in its]
        kd = [k[i] * jnp.exp(g_last[i] - g_cum_c[i]) for i in its]
        t_inv = [eye_f + x for x in pw]
        p16 = [x.astype(BF16) for x in pw]
        pw = [_dot(x, x) for x in p16]
        for _ in range(int(math.log2(C)) - 2):
            p16 = [x.astype(BF16) for x in pw]
            tp = [_dot(jnp.concatenate([t_inv[i].astype(BF16), p16[i]], axis=0), p16[i]) for i in its]
            t_inv = [t_inv[i] + tp[i][:P] for i in its]
            pw = [tp[i][P:] for i in its]
        t_inv = [t_inv[i] + _dot(t_inv[i].astype(BF16), pw[i].astype(BF16)) for i in its]
        wu16 = [_dot(t_inv[i].astype(BF16), rhs16[i]).astype(BF16) for i in its]
        qo = [_dot(qk16[i], wu16[i]) for i in its]
        kd2 = [jnp.concatenate([jnp.where(first_chunk, x, 0.0), jnp.where(first_chunk, 0.0, x)],
                               axis=1).astype(BF16) for x in kd]
        mn = [_dot_tn(kd2[i], wu16[i]) for i in its]
        for i in its:
            hh = hd[i]
            op_s[pl.ds(r0[i], P), cols[hh]] = qo[i][:, D:]
            qp16 = (q[i] * e_g[i] - qo[i][:, :D]).astype(BF16)
            for c in range(P // C):
                chunk = mp[i] * (P // C) + c
                n0 = pl.multiple_of(chunk * D, D)
                q0 = pl.multiple_of(chunk * (D + C), D + C)
                g0 = pl.multiple_of(chunk * SUBLANES, SUBLANES)
                mq_s[hh, pl.ds(q0, D), :] = (-mn[i][c * D:(c + 1) * D, :D]).astype(BF16)
                mq_s[hh, pl.ds(q0 + D, C), :] = qp16[c * C:(c + 1) * C]
                n_s[hh, pl.ds(n0, D), :] = mn[i][c * D:(c + 1) * D, D:]
                gl_s[hh, pl.ds(g0, SUBLANES), :] = jnp.broadcast_to(
                    jnp.exp(g_last[i][c * C:c * C + SUBLANES, :]), (SUBLANES, D))
        return carry

    lax.fori_loop(0, seq // (P * pairs_per_step), pair_body, 0)

    ng = ng_ref[...]

    def rec_body(n, state):
        r0 = pl.multiple_of(n * C, C)
        n0 = pl.multiple_of(n * D, D)
        q0 = pl.multiple_of(n * (D + C), D + C)
        s16 = [s.astype(BF16) for s in state]
        ms = [_dot(mq_s[hh, pl.ds(q0, D + C), :], s16[hh]) for hh in heads]
        new_state = []
        for hh in heads:
            gl = gl_s[hh, pl.ds(pl.multiple_of(n * SUBLANES, SUBLANES), 1), :]
            new_state.append(state[hh] * gl + ms[hh][:D] + n_s[hh, pl.ds(n0, D), :])
        for hh in heads:
            cs = cols[hh]
            o = ms[hh][D:] + op_s[pl.ds(r0, C), cs]
            o = o * lax.rsqrt(jnp.mean(o * o, axis=-1, keepdims=True) + RMS_EPS) * ng
            z = z_ref[pl.ds(r0, C), cs].astype(F32)
            o_ref[pl.ds(r0, C), cs] = (o * _silu(z)).astype(o_ref.dtype)
        return tuple(new_state)

    lax.fori_loop(0, n_chunks, rec_body, tuple(jnp.zeros((D, D), F32) for _ in heads), unroll=2)


def _gdn(proj3, conv_w2, a_t, b_t, a_log, dt_bias, norm_g, batch, seq, col0, hb=2):
    C = GDN_CHUNK
    H = GDN_HEADS
    D = HEAD_DIM
    P = 2 * C
    bw = hb * D
    assert H % hb == 0 and col0 % bw == 0 and seq % 256 == 0 and P == LANES
    n_hg = H // hb
    c0 = col0 // bw
    nq = H * D // bw
    n_chunks = seq // C
    n_pairs = seq // P

    def pspec(seg):
        return pl.BlockSpec((None, seq, bw), lambda b, g, *_: (b, 0, c0 + seg * nq + g))

    def cwspec(seg):
        return pl.BlockSpec((GDN_CONV, bw), lambda b, g, *_: (0, seg * nq + g))

    abspec = pl.BlockSpec((None, hb, n_pairs, P), lambda b, g, *_: (b, g, 0, 0))
    kern = functools.partial(_gdn_kernel, hb=hb, seq=seq)
    return pl.pallas_call(
        kern,
        grid_spec=pltpu.PrefetchScalarGridSpec(
            num_scalar_prefetch=2,
            grid=(batch, n_hg),
            in_specs=[pspec(0), pspec(1), pspec(2), pspec(3), cwspec(0), cwspec(1), cwspec(2),
                      abspec, abspec, pl.BlockSpec((1, D), lambda b, g, *_: (0, 0))],
            out_specs=pl.BlockSpec((None, seq, bw), lambda b, g, *_: (b, 0, g)),
            scratch_shapes=[
                pltpu.VMEM((seq + SUBLANES, bw), F32),
                pltpu.VMEM((seq, bw), F32),
                pltpu.VMEM((seq, bw), F32),
                pltpu.VMEM((seq, bw), F32),
                pltpu.VMEM((hb, n_pairs, P), F32),
                pltpu.VMEM((hb, n_chunks * (D + C), D), BF16),
                pltpu.VMEM((seq, bw), F32),
                pltpu.VMEM((hb, n_chunks * D, D), F32),
                pltpu.VMEM((hb, n_chunks * SUBLANES, D), F32),
            ]),
        out_shape=jax.ShapeDtypeStruct((batch, seq, H * D), BF16),
        compiler_params=_cparams(("parallel", "parallel")),
        name="gdn",
    )(a_log, dt_bias, proj3, proj3, proj3, proj3, conv_w2, conv_w2, conv_w2, a_t, b_t, norm_g)


def _merge_kernel(o0_ref, o1_ref, o2_ref, l0_ref, l1_ref, l2_ref, yb_ref, ga_ref, gb_ref,
                  wa_ref, wb_ref, out_ref, ya_s):
    @pl.when(pl.program_id(1) == 0)
    def _():
        l0, l1, l2 = l0_ref[...], l1_ref[...], l2_ref[...]
        m = jnp.maximum(jnp.maximum(l0, l1), l2)
        e0, e1, e2 = jnp.exp(l0 - m), jnp.exp(l1 - m), jnp.exp(l2 - m)
        inv = 1.0 / (e0 + e1 + e2)
        w0, w1, w2 = e0 * inv, e1 * inv, e2 * inv
        for h in range(SWA_HEADS_PER_GROUP):
            cs = slice(h * HEAD_DIM, (h + 1) * HEAD_DIM)
            ya = (w0[:, h:h + 1] * o0_ref[:, cs].astype(F32) + w1[:, h:h + 1] * o1_ref[:, cs].astype(F32)
                  + w2[:, h:h + 1] * o2_ref[:, cs].astype(F32))
            ya_s[:, cs] = ya.astype(BF16)

    pa = _dot(ya_s[...], wa_ref[...])
    pb = _dot(yb_ref[...], wb_ref[...])
    out_ref[...] = (_sigmoid(ga_ref[...].astype(F32)) * pa
                    + _sigmoid(gb_ref[...].astype(F32)) * pb).astype(out_ref.dtype)


def _merge(o_groups, lse_groups, y_b, gates, w_o_swa, w_o_gdn, d_model, *, tm=512, tn=512):
    t = y_b.shape[0]
    wa_k, wb_k = w_o_swa.shape[0], w_o_gdn.shape[0]
    assert t % tm == 0 and d_model % tn == 0
    nb = d_model // tn
    row = lambda w: pl.BlockSpec((tm, w), lambda i, j: (i, 0))
    return pl.pallas_call(
        _merge_kernel,
        grid=(t // tm, nb),
        in_specs=[row(wa_k), row(wa_k), row(wa_k), row(LANES), row(LANES), row(LANES), row(wb_k),
                  pl.BlockSpec((tm, tn), lambda i, j: (i, j)),
                  pl.BlockSpec((tm, tn), lambda i, j: (i, nb + j)),
                  pl.BlockSpec((wa_k, tn), lambda i, j: (0, j)),
                  pl.BlockSpec((wb_k, tn), lambda i, j: (0, j))],
        out_specs=pl.BlockSpec((tm, tn), lambda i, j: (i, j)),
        out_shape=jax.ShapeDtypeStruct((t, d_model), BF16),
        scratch_shapes=[pltpu.VMEM((tm, wa_k), BF16)],
        compiler_params=_cparams(("parallel", "arbitrary")),
        name="merge",
    )(*o_groups, *lse_groups, y_b, gates, gates, w_o_swa, w_o_gdn)


def _layer_norm(v, g, b):
    mu = jnp.mean(v, axis=-1, keepdims=True)
    c = v - mu
    var = jnp.mean(c * c, axis=-1, keepdims=True)
    return c * lax.rsqrt(var + LN_EPS) * g + b


def _proj_ln_kernel(a_ref, w_ref, res_ref, g_ref, b_ref, o_ref, op_ref, *, tm):
    k = pl.program_id(1)
    part = _dot(a_ref[...], w_ref[...])

    @pl.when(k == 0)
    def _():
        o_ref[...] = part

    @pl.when(k > 0)
    def _():
        o_ref[...] += part

    @pl.when(k == pl.num_programs(1) - 1)
    def _():
        slab = 64
        for r in range(0, tm, slab):
            rs = slice(r, r + slab)
            hn = _layer_norm(DEEPNORM_ALPHA * res_ref[rs, :] + o_ref[rs, :], g_ref[...], b_ref[...])
            o_ref[rs, :] = hn
            op_ref[rs, :] = _pack_halves(hn)


def _proj_ln(a, w, res, g, b, *, tm=512, tk=512):
    t, kdim = a.shape
    d = w.shape[1]
    assert t % tm == 0 and kdim % tk == 0
    return pl.pallas_call(
        functools.partial(_proj_ln_kernel, tm=tm),
        grid=(t // tm, kdim // tk),
        in_specs=[pl.BlockSpec((tm, tk), lambda i, k: (i, k)),
                  pl.BlockSpec((tk, d), lambda i, k: (k, 0)),
                  pl.BlockSpec((tm, d), lambda i, k: (i, 0), pipeline_mode=pl.Buffered(1)),
                  pl.BlockSpec((1, d), lambda i, k: (0, 0)),
                  pl.BlockSpec((1, d), lambda i, k: (0, 0))],
        out_specs=[pl.BlockSpec((tm, d), lambda i, k: (i, 0)),
                   pl.BlockSpec((tm, d // 2), lambda i, k: (i, 0))],
        out_shape=[jax.ShapeDtypeStruct((t, d), F32),
                   jax.ShapeDtypeStruct((t, d // 2), jnp.uint32)],
        compiler_params=_cparams(("parallel", "arbitrary")),
        name="proj_ln",
    )(a, w, res, g, b)


def _router_kernel(h_ref, w_ref, bias_ref, eidx_ref, gate_ref, pos_ref, cnt_ref, cnt_s, *, tm):
    i = pl.program_id(0)

    @pl.when(i == 0)
    def _():
        cnt_s[...] = jnp.zeros(cnt_s.shape, F32)

    scores = _sigmoid(_dot_hi(h_ref[...], w_ref[...]))
    lane = lax.broadcasted_iota(jnp.int32, (tm, LANES), 1)
    sel = jnp.where(lane < N_EXPERTS, scores + bias_ref[...], -jnp.inf)
    chosen = jnp.zeros((tm, LANES), jnp.bool_)
    eidx = jnp.zeros((tm, LANES), jnp.int32)
    gsel = jnp.zeros((tm, LANES), F32)
    picks = []
    for k in range(TOP_K):
        m = jnp.max(sel, axis=-1, keepdims=True)
        idx = jnp.min(jnp.where(sel == m, lane, LANES), axis=-1, keepdims=True)
        picks.append(idx)
        hit = lane == idx
        sc = jnp.sum(jnp.where(hit, scores, 0.0), axis=-1, keepdims=True)
        eidx = jnp.where(lane == k, idx, eidx)
        gsel = jnp.where(lane == k, sc, gsel)
        chosen = chosen | hit
        sel = jnp.where(hit, -jnp.inf, sel)
    gate_ref[...] = gsel / jnp.sum(gsel, axis=-1, keepdims=True) * ROUTED_SCALE

    ri = lax.broadcasted_iota(jnp.int32, (tm, tm), 0)
    ci = lax.broadcasted_iota(jnp.int32, (tm, tm), 1)
    ch = jnp.where(chosen, 1.0, 0.0)
    rank = _dot(jnp.where(ri > ci, 1.0, 0.0).astype(BF16), ch.astype(BF16)) + cnt_s[0:1, :]
    pos = jnp.zeros((tm, LANES), F32)
    for k in range(TOP_K):
        pk = jnp.sum(jnp.where(lane == picks[k], rank, 0.0), axis=-1, keepdims=True)
        pos = jnp.where(lane == k, pk, pos)
    pos_ref[...] = pos.astype(jnp.int32)
    eidx_ref[...] = eidx
    total = cnt_s[...] + jnp.sum(ch, axis=0, keepdims=True)
    cnt_s[...] = total
    cnt_ref[...] = total


def _router(h, w_router_p, bias_p, *, tm=256):
    t, d = h.shape
    assert t % tm == 0
    tile = pl.BlockSpec((tm, LANES), lambda i: (i, 0))
    return pl.pallas_call(
        functools.partial(_router_kernel, tm=tm),
        grid=(t // tm,),
        in_specs=[pl.BlockSpec((tm, d), lambda i: (i, 0)),
                  pl.BlockSpec((d, LANES), lambda i: (0, 0)),
                  pl.BlockSpec((1, LANES), lambda i: (0, 0))],
        out_specs=[tile, tile, tile, pl.BlockSpec((SUBLANES, LANES), lambda i: (0, 0))],
        out_shape=[jax.ShapeDtypeStruct((t, LANES), jnp.int32),
                   jax.ShapeDtypeStruct((t, LANES), F32),
                   jax.ShapeDtypeStruct((t, LANES), jnp.int32),
                   jax.ShapeDtypeStruct((SUBLANES, LANES), F32)],
        scratch_shapes=[pltpu.VMEM((SUBLANES, LANES), F32)],
        compiler_params=_cparams(("arbitrary",)),
        name="router",
    )(h, w_router_p, bias_p)


def _gather_copy(h_hbm, xbuf, sem, tok, slot, r):
    return pltpu.make_async_copy(h_hbm.at[pl.ds(tok, 1), :], xbuf.at[slot, pl.ds(r, 1), :], sem.at[slot])


def _expert_changed(i, be_ref, n_used):
    last = jnp.maximum(n_used - 1, 0)
    cur = be_ref[jnp.minimum(i, last)]
    prev = be_ref[jnp.minimum(jnp.maximum(i - 1, 0), last)]
    return (i == 0) | (cur != prev)


def _experts_up_kernel(be_ref, tok_ref, nused_ref, h_hbm, wg_ref, wu_ref, hid_ref, xbuf, wg16, wu16, sem):
    i = pl.program_id(0)
    n_used = nused_ref[0]
    slot = lax.rem(i, 2)

    def wait_slot(s):
        def wbody(r, carry):
            _gather_copy(h_hbm, xbuf, sem, 0, s, r).wait()
            return carry
        lax.fori_loop(0, MOE_ROWS, wbody, 0, unroll=8)

    @pl.when(i == 0)
    def _():
        def body(r, carry):
            _gather_copy(h_hbm, xbuf, sem, tok_ref[r], 0, r).start()
            return carry
        lax.fori_loop(0, MOE_ROWS, body, 0, unroll=8)

    @pl.when(_expert_changed(i, be_ref, n_used))
    def _():
        wg16[...] = wg_ref[...].astype(BF16)
        wu16[...] = wu_ref[...].astype(BF16)

    @pl.when(i < n_used)
    def _():
        base = jnp.minimum(i + 1, n_used - 1) * MOE_ROWS
        for r in range(MOE_ROWS):
            _gather_copy(h_hbm, xbuf, sem, tok_ref[base + r], 1 - slot, r).start()
        wait_slot(slot)
        hi, lo = _unpack_halves(xbuf[slot])
        x = jnp.concatenate([hi.astype(BF16), lo.astype(BF16)], axis=1)
        hid_ref[...] = (_silu(_dot(x, wg16[...])) * _dot(x, wu16[...])).astype(hid_ref.dtype)

    @pl.when(i == n_used - 1)
    def _():
        wait_slot(1 - slot)

    @pl.when(i >= n_used)
    def _():
        hid_ref[...] = jnp.zeros(hid_ref.shape, hid_ref.dtype)


def _experts_down_kernel(be_ref, nused_ref, hid_ref, wd_ref, y_ref, wd16):
    i = pl.program_id(0)
    n_used = nused_ref[0]

    @pl.when(_expert_changed(i, be_ref, n_used))
    def _():
        wd16[...] = wd_ref[...].astype(BF16)

    @pl.when(i < n_used)
    def _():
        y_ref[...] = _pack_halves(_dot(hid_ref[...], wd16[...]))

    @pl.when(i >= n_used)
    def _():
        y_ref[...] = jnp.zeros(y_ref.shape, y_ref.dtype)


def _experts(h, row_tok, block_e, n_used, wg, wu, wd):
    t, dh = h.shape
    d = 2 * dh
    n_blocks = block_e.shape[0]
    de = wg.shape[2]
    rows = n_blocks * MOE_ROWS

    def blk(i, nu):
        return jnp.minimum(i, jnp.maximum(nu[0] - 1, 0))

    hid = pl.pallas_call(
        _experts_up_kernel,
        grid_spec=pltpu.PrefetchScalarGridSpec(
            num_scalar_prefetch=3,
            grid=(n_blocks,),
            in_specs=[pl.BlockSpec(memory_space=pl.ANY),
                      pl.BlockSpec((None, d, de), lambda i, be, tok, nu: (be[blk(i, nu)], 0, 0)),
                      pl.BlockSpec((None, d, de), lambda i, be, tok, nu: (be[blk(i, nu)], 0, 0))],
            out_specs=pl.BlockSpec((MOE_ROWS, de), lambda i, be, tok, nu: (i, 0)),
            scratch_shapes=[pltpu.VMEM((2, MOE_ROWS, dh), jnp.uint32),
                            pltpu.VMEM((d, de), BF16), pltpu.VMEM((d, de), BF16),
                            pltpu.SemaphoreType.DMA((2,))]),
        out_shape=jax.ShapeDtypeStruct((rows, de), BF16),
        compiler_params=_cparams(("arbitrary",)),
        name="experts_up",
    )(block_e, row_tok, n_used, h, wg, wu)
    return pl.pallas_call(
        _experts_down_kernel,
        grid_spec=pltpu.PrefetchScalarGridSpec(
            num_scalar_prefetch=2,
            grid=(n_blocks,),
            in_specs=[pl.BlockSpec((MOE_ROWS, de), lambda i, be, nu: (i, 0)),
                      pl.BlockSpec((None, de, d), lambda i, be, nu: (be[blk(i, nu)], 0, 0))],
            out_specs=pl.BlockSpec((MOE_ROWS, dh), lambda i, be, nu: (i, 0)),
            scratch_shapes=[pltpu.VMEM((de, d), BF16)]),
        out_shape=jax.ShapeDtypeStruct((rows, dh), jnp.uint32),
        compiler_params=_cparams(("arbitrary",)),
        name="experts_down",
    )(block_e, n_used, hid, wd)


def _shared_kernel(h_ref, wg_ref, wu_ref, wd_ref, o_ref):
    x = h_ref[...].astype(BF16)
    hid = _silu(_dot(x, wg_ref[...])) * _dot(x, wu_ref[...])
    o_ref[...] = _dot(hid.astype(BF16), wd_ref[...])


def _shared(h, wg, wu, wd, *, tm=512):
    t, d = h.shape
    ds_ = wg.shape[1]
    return pl.pallas_call(
        _shared_kernel,
        grid=(t // tm,),
        in_specs=[pl.BlockSpec((tm, d), lambda i: (i, 0)),
                  pl.BlockSpec((d, ds_), lambda i: (0, 0)),
                  pl.BlockSpec((d, ds_), lambda i: (0, 0)),
                  pl.BlockSpec((ds_, d), lambda i: (0, 0))],
        out_specs=pl.BlockSpec((tm, d), lambda i: (i, 0)),
        out_shape=jax.ShapeDtypeStruct((t, d), F32),
        compiler_params=_cparams(("parallel",)),
        name="shared_expert",
    )(h, wg, wu, wd)


def _combine_copy(y_hbm, ybuf, sem, row, slot, k, r):
    return pltpu.make_async_copy(y_hbm.at[pl.ds(row, 1), :], ybuf.at[slot, k, pl.ds(r, 1), :], sem.at[slot])


def _combine_kernel(dest_ref, y_hbm, h_ref, sh_ref, gate_ref, g_ref, b_ref, o_ref, ybuf, sem, *, tm):
    i = pl.program_id(0)
    n = pl.num_programs(0)
    slot = lax.rem(i, 2)

    def issue(tile, s):
        def body(r, carry):
            base = (tile * tm + r) * TOP_K
            for k in range(TOP_K):
                _combine_copy(y_hbm, ybuf, sem, dest_ref[base + k], s, k, r).start()
            return carry
        lax.fori_loop(0, tm, body, 0)

    @pl.when(i == 0)
    def _():
        issue(0, 0)

    @pl.when(i + 1 < n)
    def _():
        issue(i + 1, 1 - slot)

    def wbody(r, carry):
        for k in range(TOP_K):
            _combine_copy(y_hbm, ybuf, sem, 0, slot, k, r).wait()
        return carry
    lax.fori_loop(0, tm, wbody, 0)

    slab = 32
    dh = h_ref.shape[1] // 2
    for r in range(0, tm, slab):
        rs = slice(r, r + slab)
        gate = gate_ref[rs, :]
        res = DEEPNORM_ALPHA * h_ref[rs, :] + sh_ref[rs, :]
        acc_hi, acc_lo = res[:, :dh], res[:, dh:]
        for k in range(TOP_K):
            hi, lo = _unpack_halves(ybuf[slot, k, rs, :])
            acc_hi = acc_hi + gate[:, k:k + 1] * hi
            acc_lo = acc_lo + gate[:, k:k + 1] * lo
        o_ref[rs, :] = _layer_norm(jnp.concatenate([acc_hi, acc_lo], axis=1), g_ref[...], b_ref[...])


def _combine(dest_flat, y_rows, h, shared, gate, g, b, *, tm=128):
    t, d = h.shape
    assert t % tm == 0
    return pl.pallas_call(
        functools.partial(_combine_kernel, tm=tm),
        grid_spec=pltpu.PrefetchScalarGridSpec(
            num_scalar_prefetch=1,
            grid=(t // tm,),
            in_specs=[pl.BlockSpec(memory_space=pl.ANY),
                      pl.BlockSpec((tm, d), lambda i, dr: (i, 0)),
                      pl.BlockSpec((tm, d), lambda i, dr: (i, 0)),
                      pl.BlockSpec((tm, LANES), lambda i, dr: (i, 0)),
                      pl.BlockSpec((1, d), lambda i, dr: (0, 0)),
                      pl.BlockSpec((1, d), lambda i, dr: (0, 0))],
            out_specs=pl.BlockSpec((tm, d), lambda i, dr: (i, 0)),
            scratch_shapes=[pltpu.VMEM((2, TOP_K, tm, d // 2), jnp.uint32), pltpu.SemaphoreType.DMA((2,))]),
        out_shape=jax.ShapeDtypeStruct((t, d), F32),
        compiler_params=_cparams(("arbitrary",)),
        name="combine",
    )(dest_flat, y_rows, h, shared, gate, g, b)


def _layer(x, w_in, conv_w, a_log, dt_bias, norm_g, w_o_swa, w_o_gdn, w_out, ln1_g, ln1_b,
           w_router, router_bias, w_e_gate, w_e_up, w_e_down, w_s_gate, w_s_up, w_s_down, ln2_g, ln2_b):
    batch, seq, d = x.shape
    t = batch * seq
    n_groups = len(SWA_GROUPS)
    swa_w = n_groups * SWA_HEADS_PER_GROUP * HEAD_DIM
    gdn_w = GDN_HEADS * HEAD_DIM
    main_w = 3 * swa_w + 4 * gdn_w
    ba_w = 2 * GDN_HEADS
    assert w_in.shape[1] == main_w + ba_w + 2 * d

    xf = x.reshape(t, d)
    x16 = xf.astype(BF16)

    tn = 512
    gw = SWA_HEADS_PER_GROUP * HEAD_DIM
    assert gw % tn == 0 and (3 * swa_w) % tn == 0 and (4 * gdn_w) % tn == 0
    bpg = gw // tn

    w_in_t = w_in.T

    o_groups, lse_groups = [], []
    for j, (_, dil) in enumerate(SWA_GROUPS):
        col_map = lambda c, j=j: ((c // bpg) * n_groups + j) * bpg + c % bpg
        if dil == 1:
            qkv = _proj(x16, w_in_t, 3 * gw, tm=1024, tn=tn, out_dtype=BF16, col_map=col_map,
                        name=f"in_proj_swa{j}")
        else:
            qkv = _proj_residue(x16, w_in_t, 3 * gw, batch, seq, dil, tm=1024, tn=tn, out_dtype=BF16,
                                col_map=col_map, name=f"in_proj_swa{j}")
        o, lse = _swa_group(qkv, j, batch, seq)
        o_groups.append(o)
        lse_groups.append(lse)

    gdn_in = _proj(x16, w_in_t, 4 * gdn_w, tm=1024, tn=tn, out_dtype=BF16, name="in_proj_gdn",
                   col_map=lambda c: c + 3 * swa_w // tn)
    w_ba = jnp.pad(w_in_t[main_w:main_w + ba_w], ((0, LANES - ba_w), (0, 0)))
    ba = _proj(x16, w_ba, LANES, tm=1024, tn=LANES, out_dtype=F32, name="in_proj_ba")
    gates = _proj(x16, w_in_t[main_w + ba_w:], 2 * d, tm=1024, tn=tn, out_dtype=BF16, name="in_proj_gates")
    pair = 2 * GDN_CHUNK
    ba3 = ba[:, :ba_w].reshape(batch, seq // pair, pair, ba_w).transpose(0, 3, 1, 2)
    b_t, a_t = ba3[:, :GDN_HEADS], ba3[:, GDN_HEADS:]
    y_b = _gdn(gdn_in.reshape(batch, seq, 4 * gdn_w), conv_w.reshape(GDN_CONV, 3 * gdn_w), a_t, b_t,
               a_log.astype(F32), dt_bias.astype(F32), norm_g.reshape(1, HEAD_DIM).astype(F32), batch, seq, 0)
    y_b = y_b.reshape(t, gdn_w)

    merged = _merge(o_groups, lse_groups, y_b, gates, w_o_swa.astype(BF16), w_o_gdn.astype(BF16), d)
    h, h_packed = _proj_ln(merged, w_out.astype(BF16), xf, ln1_g.reshape(1, d), ln1_b.reshape(1, d))

    e_pad = LANES - N_EXPERTS
    eidx, gate, pos, cnt = _router(h, jnp.pad(w_router, ((0, 0), (0, e_pad))),
                                   jnp.pad(router_bias.reshape(1, N_EXPERTS), ((0, 0), (0, e_pad))))
    counts = cnt[0, :N_EXPERTS].astype(jnp.int32)
    padded = (counts + MOE_ROWS - 1) // MOE_ROWS * MOE_ROWS
    pend = jnp.cumsum(padded)
    pstart = pend - padded
    e_sel = eidx[:, :TOP_K, None] == jnp.arange(N_EXPERTS, dtype=jnp.int32)
    dest = jnp.sum(jnp.where(e_sel, pstart, 0), axis=-1) + pos[:, :TOP_K]
    n_blocks = (t * TOP_K + N_EXPERTS * (MOE_ROWS - 1) + MOE_ROWS - 1) // MOE_ROWS
    dest_flat = dest.reshape(-1)
    tok_of = jnp.arange(t * TOP_K, dtype=jnp.int32) // TOP_K
    row_tok = jnp.zeros((n_blocks * MOE_ROWS,), jnp.int32).at[dest_flat].set(tok_of)
    block_e = jnp.minimum(jnp.searchsorted(pend, jnp.arange(n_blocks, dtype=jnp.int32) * MOE_ROWS, side='right'),
                          N_EXPERTS - 1).astype(jnp.int32)
    n_used = (pend[-1:] // MOE_ROWS).astype(jnp.int32)
    y_rows = _experts(h_packed, row_tok, block_e, n_used, w_e_gate, w_e_up, w_e_down)
    shared = _shared(h, w_s_gate.astype(BF16), w_s_up.astype(BF16), w_s_down.astype(BF16))
    out = _combine(dest_flat, y_rows, h, shared, gate, ln2_g.reshape(1, d), ln2_b.reshape(1, d))
    return out.reshape(batch, seq, d)


@jax.jit
def kernel(x, w_in, conv_w, gdn_a_log, gdn_dt_bias, gdn_norm_g, w_o_swa, w_o_gdn, w_out, ln1_g, ln1_b,
           w_router, router_bias, w_e_gate, w_e_up, w_e_down, w_s_gate, w_s_up, w_s_down, ln2_g, ln2_b):
    for l in range(DEPTH):
        x = _layer(x, w_in[l], conv_w[l], gdn_a_log[l], gdn_dt_bias[l], gdn_norm_g[l], w_o_swa[l],
                   w_o_gdn[l], w_out[l], ln1_g[l], ln1_b[l], w_router[l], router_bias[l], w_e_gate[l],
                   w_e_up[l], w_e_down[l], w_s_gate[l], w_s_up[l], w_s_down[l], ln2_g[l], ln2_b[l])
    return x
```

```python
import functools
import math

import jax
import jax.numpy as jnp
from jax import lax
from jax.experimental import pallas as pl
from jax.experimental.pallas import tpu as pltpu

HEAD_DIM = 128
SWA_GROUPS = ((128, 1), (512, 4), (2048, 16))
SWA_HEADS_PER_GROUP = 8
GDN_HEADS = 16
GDN_CONV = 4
GDN_CHUNK = 64
N_EXPERTS = 96
TOP_K = 8
ROUTED_SCALE = 2.5
LN_EPS = 1e-5
RMS_EPS = 1e-6
DEPTH = 1
DEEPNORM_ALPHA = (2 * DEPTH) ** 0.25

LANES = 128
SUBLANES = 8
VMEM_LIMIT = 58 * 1024 * 1024
MOE_ROWS = 256

F32 = jnp.float32
BF16 = jnp.bfloat16
HIGHEST = lax.Precision.HIGHEST


def _cparams(sem):
    return pltpu.CompilerParams(dimension_semantics=sem, vmem_limit_bytes=VMEM_LIMIT)


def _dot(a, b):
    return jnp.dot(a, b, preferred_element_type=F32)


def _dot_nt(a, b):
    return lax.dot_general(a, b, (((1,), (1,)), ((), ())), preferred_element_type=F32)


def _dot_tn(a, b):
    return lax.dot_general(a, b, (((0,), (0,)), ((), ())), preferred_element_type=F32)


def _dot_hi(a, b):
    return jnp.dot(a, b, preferred_element_type=F32, precision=HIGHEST)


def _sigmoid(x):
    return 1.0 / (1.0 + jnp.exp(-x))


def _silu(x):
    return x * _sigmoid(x)


def _pack_halves(v):
    n = v.shape[1] // 2
    hi = pltpu.bitcast(v[:, :n].astype(BF16).astype(F32), jnp.uint32)
    lo = pltpu.bitcast(v[:, n:].astype(BF16).astype(F32), jnp.uint32)
    return hi | (lo >> 16)


def _unpack_halves(p):
    hi = pltpu.bitcast(p & jnp.uint32(0xFFFF0000), F32)
    lo = pltpu.bitcast(p << 16, F32)
    return hi, lo


def _proj_kernel(a_ref, bt_ref, o_ref):
    o_ref[...] = _dot_nt(a_ref[...].astype(BF16), bt_ref[...].astype(BF16)).astype(o_ref.dtype)


def _proj(a, bt, n_cols, *, tm, tn, out_dtype, col_map=lambda j: j, name="proj"):
    m, k = a.shape
    assert m % tm == 0 and n_cols % tn == 0 and bt.shape[1] == k
    return pl.pallas_call(
        _proj_kernel,
        grid=(m // tm, n_cols // tn),
        in_specs=[pl.BlockSpec((tm, k), lambda i, j: (i, 0)),
                  pl.BlockSpec((tn, k), lambda i, j: (col_map(j), 0))],
        out_specs=pl.BlockSpec((tm, tn), lambda i, j: (i, j)),
        out_shape=jax.ShapeDtypeStruct((m, n_cols), out_dtype),
        compiler_params=_cparams(("parallel", "arbitrary")),
        name=name,
    )(a, bt)


def _proj_residue_kernel(a_ref, bt_ref, o_ref, y_s, *, dil):
    y = _dot_nt(a_ref[...].astype(BF16), bt_ref[...].astype(BF16))
    rows = y_s.shape[1] // dil
    for c in range(y_s.shape[0]):
        cs = slice(c * LANES, (c + 1) * LANES)
        y_s[c] = y[:, cs]
        for r in range(dil):
            o_ref[r, :, cs] = y_s[c, pl.ds(r, rows, stride=dil), :].astype(o_ref.dtype)


def _proj_residue(a, bt, n_cols, batch, seq, dil, *, tm, tn, out_dtype, col_map, name):
    m, k = a.shape
    assert seq % tm == 0 and tm % (dil * 16) == 0 and n_cols % tn == 0 and bt.shape[1] == k
    spb = seq // tm
    out = pl.pallas_call(
        functools.partial(_proj_residue_kernel, dil=dil),
        grid=(m // tm, n_cols // tn),
        in_specs=[pl.BlockSpec((tm, k), lambda i, j: (i, 0)),
                  pl.BlockSpec((tn, k), lambda i, j: (col_map(j), 0))],
        out_specs=pl.BlockSpec((None, dil, tm // dil, tn), lambda i, j: (i // spb, 0, i % spb, j)),
        out_shape=jax.ShapeDtypeStruct((batch, dil, seq // dil, n_cols), out_dtype),
        scratch_shapes=[pltpu.VMEM((tn // LANES, tm, LANES), F32)],
        compiler_params=_cparams(("parallel", "arbitrary")),
        name=name,
    )(a, bt)
    return out.reshape(m, n_cols)


def _swa_kernel(q_ref, k_ref, v_ref, o_ref, lse_ref, *, n_heads, span, n_blocks):
    scale = HEAD_DIM ** -0.5
    qi = lax.broadcasted_iota(jnp.int32, (span, 2 * span), 0)
    ci = lax.broadcasted_iota(jnp.int32, (span, 2 * span), 1)
    mask_win = (ci >= qi) & (ci <= qi + span)
    qi0 = lax.broadcasted_iota(jnp.int32, (span, span), 0)
    ci0 = lax.broadcasted_iota(jnp.int32, (span, span), 1)
    mask_first = ci0 <= qi0
    lane = lax.broadcasted_iota(jnp.int32, (span, LANES), 1)

    def attend(q, kw, vw, mask):
        s = _dot_nt(q, kw) * scale
        s = jnp.where(mask, s, -jnp.inf)
        m = jnp.max(s, axis=-1, keepdims=True)
        p = jnp.exp(s - m)
        den = jnp.sum(p, axis=-1, keepdims=True)
        o = _dot(p.astype(BF16), vw) / den
        return o, m + jnp.log(den)

    def block(q0, k0, klen, mask):
        lse_acc = jnp.zeros((span, LANES), F32)
        for h in range(n_heads):
            cs = slice(h * HEAD_DIM, (h + 1) * HEAD_DIM)
            o, lse = attend(q_ref[pl.ds(q0, span), cs], k_ref[pl.ds(k0, klen), cs],
                            v_ref[pl.ds(k0, klen), cs], mask)
            o_ref[pl.ds(q0, span), cs] = o.astype(o_ref.dtype)
            lse_acc = jnp.where(lane == h, lse, lse_acc)
        lse_ref[pl.ds(q0, span), :] = lse_acc

    block(0, 0, span, mask_first)

    def body(n, carry):
        q0 = pl.multiple_of(n * span, span)
        k0 = pl.multiple_of((n - 1) * span, span)
        block(q0, k0, 2 * span, mask_win)
        return carry

    lax.fori_loop(1, n_blocks, body, 0)


def _swa_group(qkv, group, batch, seq):
    window, dil = SWA_GROUPS[group]
    span = window // dil
    sub_len = seq // dil
    assert seq % dil == 0 and sub_len % span == 0 and span % LANES == 0
    gw = SWA_HEADS_PER_GROUP * HEAD_DIM
    view = qkv.reshape(batch, dil, sub_len, 3 * gw)
    kern = functools.partial(_swa_kernel, n_heads=SWA_HEADS_PER_GROUP, span=span,
                             n_blocks=sub_len // span)

    def spec(seg):
        return pl.BlockSpec((None, None, sub_len, gw), lambda b, r: (b, r, 0, seg))

    o, lse = pl.pallas_call(
        kern,
        grid=(batch, dil),
        in_specs=[spec(0), spec(1), spec(2)],
        out_specs=[pl.BlockSpec((None, sub_len, gw), lambda b, r: (b, 0, r)),
                   pl.BlockSpec((None, sub_len, LANES), lambda b, r: (b, 0, r))],
        out_shape=[jax.ShapeDtypeStruct((batch, sub_len, dil * gw), BF16),
                   jax.ShapeDtypeStruct((batch, sub_len, dil * LANES), F32)],
        compiler_params=_cparams(("parallel", "parallel")),
        name=f"swa_g{group}",
    )(view, view, view)
    return o.reshape(batch * seq, gw), lse.reshape(batch * seq, LANES)


def _gdn_kernel(alog_ref, dtb_ref, q_ref, k_ref, v_ref, z_ref, cwq_ref, cwk_ref, cwv_ref,
                a_ref, b_ref, ng_ref, o_ref,
                xpad, qn, kn, vn, gcum_s, mq_s, op_s, n_s, gl_s, *, hb, seq):
    C = GDN_CHUNK
    P = 2 * C
    D = HEAD_DIM
    n_chunks = seq // C
    hg = pl.program_id(1)
    pad = SUBLANES

    xpad[0:pad, :] = jnp.zeros((pad, hb * D), F32)
    rows = 256
    for src_ref, cw_ref, dst, mode in ((q_ref, cwq_ref, qn, "q"), (k_ref, cwk_ref, kn, "k"),
                                       (v_ref, cwv_ref, vn, "v")):
        xpad[pad:pad + seq, :] = src_ref[...].astype(F32)
        cw = cw_ref[...]
        for t0 in range(0, seq, rows):
            acc = xpad[pad + t0:pad + t0 + rows, :] * cw[GDN_CONV - 1:GDN_CONV, :]
            for j in range(GDN_CONV - 1):
                sh = GDN_CONV - 1 - j
                acc = acc + xpad[pad + t0 - sh:pad + t0 - sh + rows, :] * cw[j:j + 1, :]
            y = _silu(acc)
            for hh in range(hb):
                cs = slice(hh * D, (hh + 1) * D)
                yh = y[:, cs]
                if mode != "v":
                    yh = yh * lax.rsqrt(jnp.sum(yh * yh, axis=-1, keepdims=True) + 1e-6)
                if mode == "q":
                    yh = yh * (D ** -0.5)
                dst[t0:t0 + rows, cs] = yh

    ri = lax.broadcasted_iota(jnp.int32, (P, P), 0)
    ci = lax.broadcasted_iota(jnp.int32, (P, P), 1)
    same = (ri // C) == (ci // C)
    tri = same & (ri >= ci)
    strict = same & (ri > ci)
    eye = ri == ci
    last = ci == (ri | (C - 1))
    cum_ones = (same & (ri <= ci)).astype(F32)
    eye_f = eye.astype(F32)
    first_chunk = lax.broadcasted_iota(jnp.int32, (P, D), 0) < C

    for hh in range(hb):
        head = hg * hb + hh
        sp_in = a_ref[hh] + dtb_ref[head]
        softplus = jnp.maximum(sp_in, 0.0) + jnp.log(1.0 + jnp.exp(-jnp.abs(sp_in)))
        gcum_s[hh] = _dot_hi(-jnp.exp(alog_ref[head]) * softplus, cum_ones)

    heads = range(hb)
    cols = [slice(hh * D, (hh + 1) * D) for hh in heads]

    pairs_per_step = 2
    items = [(g, hh) for g in range(pairs_per_step) for hh in heads]
    its = range(len(items))

    def pair_body(mb, carry):
        mp = [mb * pairs_per_step + g for g, _ in items]
        r0 = [pl.multiple_of(m * P, P) for m in mp]
        hd = [hh for _, hh in items]
        g_cum_r = [jnp.broadcast_to(gcum_s[hd[i], pl.ds(mp[i], 1), :], (P, P)) for i in its]
        beta_r = [jnp.broadcast_to(_sigmoid(b_ref[hd[i], pl.ds(mp[i], 1), :]), (P, P)) for i in its]
        q = [qn[pl.ds(r0[i], P), cols[hd[i]]] for i in its]
        k = [kn[pl.ds(r0[i], P), cols[hd[i]]] for i in its]
        v = [vn[pl.ds(r0[i], P), cols[hd[i]]] for i in its]
        g_cum_c = [jnp.sum(jnp.where(eye, g, 0.0), axis=1, keepdims=True) for g in g_cum_r]
        beta_c = [jnp.sum(jnp.where(eye, b, 0.0), axis=1, keepdims=True) for b in beta_r]
        g_last = [jnp.sum(jnp.where(last, g, 0.0), axis=1, keepdims=True) for g in g_cum_r]
        gam = [jnp.exp(jnp.where(tri, g_cum_c[i] - g_cum_r[i], -jnp.inf)) for i in its]
        kb = [k[i] * beta_c[i] for i in its]
        k16 = [x.astype(BF16) for x in k]
        aq = [_dot_nt(jnp.concatenate([kb[i].astype(BF16), q[i].astype(BF16)], axis=0), k16[i])
              for i in its]
        pw = [-jnp.where(strict, aq[i][:P] * gam[i], 0.0) for i in its]
        qk16 = [jnp.where(tri, aq[i][P:] * gam[i], 0.0).astype(BF16) for i in its]
        e_g = [jnp.exp(g) for g in g_cum_c]
        rhs16 = [jnp.concatenate([kb[i] * e_g[i], v[i] * beta_c[i]], axis=1).astype(BF16)
                 for i in its]
        kd = [k[i] * jnp.exp(g_last[i] - g_cum_c[i]) for i in its]
        t_inv = [eye_f + x for x in pw]
        p16 = [x.astype(BF16) for x in pw]
        pw = [_dot(x, x) for x in p16]
        for _ in range(int(math.log2(C)) - 2):
            p16 = [x.astype(BF16) for x in pw]
            tp = [_dot(jnp.concatenate([t_inv[i].astype(BF16), p16[i]], axis=0), p16[i]) for i in its]
            t_inv = [t_inv[i] + tp[i][:P] for i in its]
            pw = [tp[i][P:] for i in its]
        t_inv = [t_inv[i] + _dot(t_inv[i].astype(BF16), pw[i].astype(BF16)) for i in its]
        wu16 = [_dot(t_inv[i].astype(BF16), rhs16[i]).astype(BF16) for i in its]
        qo = [_dot(qk16[i], wu16[i]) for i in its]
        kd2 = [jnp.concatenate([jnp.where(first_chunk, x, 0.0), jnp.where(first_chunk, 0.0, x)],
                               axis=1).astype(BF16) for x in kd]
        mn = [_dot_tn(kd2[i], wu16[i]) for i in its]
        for i in its:
            hh = hd[i]
            op_s[pl.ds(r0[i], P), cols[hh]] = qo[i][:, D:]
            qp16 = (q[i] * e_g[i] - qo[i][:, :D]).astype(BF16)
            for c in range(P // C):
                chunk = mp[i] * (P // C) + c
                n0 = pl.multiple_of(chunk * D, D)
                q0 = pl.multiple_of(chunk * (D + C), D + C)
                g0 = pl.multiple_of(chunk * SUBLANES, SUBLANES)
                mq_s[hh, pl.ds(q0, D), :] = (-mn[i][c * D:(c + 1) * D, :D]).astype(BF16)
                mq_s[hh, pl.ds(q0 + D, C), :] = qp16[c * C:(c + 1) * C]
                n_s[hh, pl.ds(n0, D), :] = mn[i][c * D:(c + 1) * D, D:]
                gl_s[hh, pl.ds(g0, SUBLANES), :] = jnp.broadcast_to(
                    jnp.exp(g_last[i][c * C:c * C + SUBLANES, :]), (SUBLANES, D))
        return carry

    lax.fori_loop(0, seq // (P * pairs_per_step), pair_body, 0)

    ng = ng_ref[...]

    def rec_body(n, state):
        r0 = pl.multiple_of(n * C, C)
        n0 = pl.multiple_of(n * D, D)
        q0 = pl.multiple_of(n * (D + C), D + C)
        s16 = [s.astype(BF16) for s in state]
        ms = [_dot(mq_s[hh, pl.ds(q0, D + C), :], s16[hh]) for hh in heads]
        new_state = []
        for hh in heads:
            gl = gl_s[hh, pl.ds(pl.multiple_of(n * SUBLANES, SUBLANES), 1), :]
            new_state.append(state[hh] * gl + ms[hh][:D] + n_s[hh, pl.ds(n0, D), :])
        for hh in heads:
            cs = cols[hh]
            o = ms[hh][D:] + op_s[pl.ds(r0, C), cs]
            o = o * lax.rsqrt(jnp.mean(o * o, axis=-1, keepdims=True) + RMS_EPS) * ng
            z = z_ref[pl.ds(r0, C), cs].astype(F32)
            o_ref[pl.ds(r0, C), cs] = (o * _silu(z)).astype(o_ref.dtype)
        return tuple(new_state)

    lax.fori_loop(0, n_chunks, rec_body, tuple(jnp.zeros((D, D), F32) for _ in heads), unroll=2)


def _gdn(proj3, conv_w2, a_t, b_t, a_log, dt_bias, norm_g, batch, seq, col0, hb=2):
    C = GDN_CHUNK
    H = GDN_HEADS
    D = HEAD_DIM
    P = 2 * C
    bw = hb * D
    assert H % hb == 0 and col0 % bw == 0 and seq % 256 == 0 and P == LANES
    n_hg = H // hb
    c0 = col0 // bw
    nq = H * D // bw
    n_chunks = seq // C
    n_pairs = seq // P

    def pspec(seg):
        return pl.BlockSpec((None, seq, bw), lambda b, g, *_: (b, 0, c0 + seg * nq + g))

    def cwspec(seg):
        return pl.BlockSpec((GDN_CONV, bw), lambda b, g, *_: (0, seg * nq + g))

    abspec = pl.BlockSpec((None, hb, n_pairs, P), lambda b, g, *_: (b, g, 0, 0))
    kern = functools.partial(_gdn_kernel, hb=hb, seq=seq)
    return pl.pallas_call(
        kern,
        grid_spec=pltpu.PrefetchScalarGridSpec(
            num_scalar_prefetch=2,
            grid=(batch, n_hg),
            in_specs=[pspec(0), pspec(1), pspec(2), pspec(3), cwspec(0), cwspec(1), cwspec(2),
                      abspec, abspec, pl.BlockSpec((1, D), lambda b, g, *_: (0, 0))],
            out_specs=pl.BlockSpec((None, seq, bw), lambda b, g, *_: (b, 0, g)),
            scratch_shapes=[
                pltpu.VMEM((seq + SUBLANES, bw), F32),
                pltpu.VMEM((seq, bw), F32),
                pltpu.VMEM((seq, bw), F32),
                pltpu.VMEM((seq, bw), F32),
                pltpu.VMEM((hb, n_pairs, P), F32),
                pltpu.VMEM((hb, n_chunks * (D + C), D), BF16),
                pltpu.VMEM((seq, bw), F32),
                pltpu.VMEM((hb, n_chunks * D, D), F32),
                pltpu.VMEM((hb, n_chunks * SUBLANES, D), F32),
            ]),
        out_shape=jax.ShapeDtypeStruct((batch, seq, H * D), BF16),
        compiler_params=_cparams(("parallel", "parallel")),
        name="gdn",
    )(a_log, dt_bias, proj3, proj3, proj3, proj3, conv_w2, conv_w2, conv_w2, a_t, b_t, norm_g)


def _merge_kernel(o0_ref, o1_ref, o2_ref, l0_ref, l1_ref, l2_ref, yb_ref, ga_ref, gb_ref,
                  wa_ref, wb_ref, out_ref, ya_s):
    @pl.when(pl.program_id(1) == 0)
    def _():
        l0, l1, l2 = l0_ref[...], l1_ref[...], l2_ref[...]
        m = jnp.maximum(jnp.maximum(l0, l1), l2)
        e0, e1, e2 = jnp.exp(l0 - m), jnp.exp(l1 - m), jnp.exp(l2 - m)
        inv = 1.0 / (e0 + e1 + e2)
        w0, w1, w2 = e0 * inv, e1 * inv, e2 * inv
        for h in range(SWA_HEADS_PER_GROUP):
            cs = slice(h * HEAD_DIM, (h + 1) * HEAD_DIM)
            ya = (w0[:, h:h + 1] * o0_ref[:, cs].astype(F32) + w1[:, h:h + 1] * o1_ref[:, cs].astype(F32)
                  + w2[:, h:h + 1] * o2_ref[:, cs].astype(F32))
            ya_s[:, cs] = ya.astype(BF16)

    pa = _dot(ya_s[...], wa_ref[...])
    pb = _dot(yb_ref[...], wb_ref[...])
    out_ref[...] = (_sigmoid(ga_ref[...].astype(F32)) * pa
                    + _sigmoid(gb_ref[...].astype(F32)) * pb).astype(out_ref.dtype)


def _merge(o_groups, lse_groups, y_b, gates, w_o_swa, w_o_gdn, d_model, *, tm=512, tn=512):
    t = y_b.shape[0]
    wa_k, wb_k = w_o_swa.shape[0], w_o_gdn.shape[0]
    assert t % tm == 0 and d_model % tn == 0
    nb = d_model // tn
    row = lambda w: pl.BlockSpec((tm, w), lambda i, j: (i, 0))
    return pl.pallas_call(
        _merge_kernel,
        grid=(t // tm, nb),
        in_specs=[row(wa_k), row(wa_k), row(wa_k), row(LANES), row(LANES), row(LANES), row(wb_k),
                  pl.BlockSpec((tm, tn), lambda i, j: (i, j)),
                  pl.BlockSpec((tm, tn), lambda i, j: (i, nb + j)),
                  pl.BlockSpec((wa_k, tn), lambda i, j: (0, j)),
                  pl.BlockSpec((wb_k, tn), lambda i, j: (0, j))],
        out_specs=pl.BlockSpec((tm, tn), lambda i, j: (i, j)),
        out_shape=jax.ShapeDtypeStruct((t, d_model), BF16),
        scratch_shapes=[pltpu.VMEM((tm, wa_k), BF16)],
        compiler_params=_cparams(("parallel", "arbitrary")),
        name="merge",
    )(*o_groups, *lse_groups, y_b, gates, gates, w_o_swa, w_o_gdn)


def _layer_norm(v, g, b):
    mu = jnp.mean(v, axis=-1, keepdims=True)
    c = v - mu
    var = jnp.mean(c * c, axis=-1, keepdims=True)
    return c * lax.rsqrt(var + LN_EPS) * g + b


def _proj_ln_kernel(a_ref, w_ref, res_ref, g_ref, b_ref, o_ref, op_ref, *, tm):
    k = pl.program_id(1)
    part = _dot(a_ref[...], w_ref[...])

    @pl.when(k == 0)
    def _():
        o_ref[...] = part

    @pl.when(k > 0)
    def _():
        o_ref[...] += part

    @pl.when(k == pl.num_programs(1) - 1)
    def _():
        slab = 64
        for r in range(0, tm, slab):
            rs = slice(r, r + slab)
            hn = _layer_norm(DEEPNORM_ALPHA * res_ref[rs, :] + o_ref[rs, :], g_ref[...], b_ref[...])
            o_ref[rs, :] = hn
            op_ref[rs, :] = _pack_halves(hn)


def _proj_ln(a, w, res, g, b, *, tm=512, tk=512):
    t, kdim = a.shape
    d = w.shape[1]
    assert t % tm == 0 and kdim % tk == 0
    return pl.pallas_call(
        functools.partial(_proj_ln_kernel, tm=tm),
        grid=(t // tm, kdim // tk),
        in_specs=[pl.BlockSpec((tm, tk), lambda i, k: (i, k)),
                  pl.BlockSpec((tk, d), lambda i, k: (k, 0)),
                  pl.BlockSpec((tm, d), lambda i, k: (i, 0), pipeline_mode=pl.Buffered(1)),
                  pl.BlockSpec((1, d), lambda i, k: (0, 0)),
                  pl.BlockSpec((1, d), lambda i, k: (0, 0))],
        out_specs=[pl.BlockSpec((tm, d), lambda i, k: (i, 0)),
                   pl.BlockSpec((tm, d // 2), lambda i, k: (i, 0))],
        out_shape=[jax.ShapeDtypeStruct((t, d), F32),
                   jax.ShapeDtypeStruct((t, d // 2), jnp.uint32)],
        compiler_params=_cparams(("parallel", "arbitrary")),
        name="proj_ln",
    )(a, w, res, g, b)


def _router_kernel(h_ref, w_ref, bias_ref, eidx_ref, gate_ref, pos_ref, cnt_ref, cnt_s, *, tm):
    i = pl.program_id(0)

    @pl.when(i == 0)
    def _():
        cnt_s[...] = jnp.zeros(cnt_s.shape, F32)

    scores = _sigmoid(_dot_hi(h_ref[...], w_ref[...]))
    lane = lax.broadcasted_iota(jnp.int32, (tm, LANES), 1)
    sel = jnp.where(lane < N_EXPERTS, scores + bias_ref[...], -jnp.inf)
    chosen = jnp.zeros((tm, LANES), jnp.bool_)
    eidx = jnp.zeros((tm, LANES), jnp.int32)
    gsel = jnp.zeros((tm, LANES), F32)
    picks = []
    for k in range(TOP_K):
        m = jnp.max(sel, axis=-1, keepdims=True)
        idx = jnp.min(jnp.where(sel == m, lane, LANES), axis=-1, keepdims=True)
        picks.append(idx)
        hit = lane == idx
        sc = jnp.sum(jnp.where(hit, scores, 0.0), axis=-1, keepdims=True)
        eidx = jnp.where(lane == k, idx, eidx)
        gsel = jnp.where(lane == k, sc, gsel)
        chosen = chosen | hit
        sel = jnp.where(hit, -jnp.inf, sel)
    gate_ref[...] = gsel / jnp.sum(gsel, axis=-1, keepdims=True) * ROUTED_SCALE

    ri = lax.broadcasted_iota(jnp.int32, (tm, tm), 0)
    ci = lax.broadcasted_iota(jnp.int32, (tm, tm), 1)
    ch = jnp.where(chosen, 1.0, 0.0)
    rank = _dot(jnp.where(ri > ci, 1.0, 0.0).astype(BF16), ch.astype(BF16)) + cnt_s[0:1, :]
    pos = jnp.zeros((tm, LANES), F32)
    for k in range(TOP_K):
        pk = jnp.sum(jnp.where(lane == picks[k], rank, 0.0), axis=-1, keepdims=True)
        pos = jnp.where(lane == k, pk, pos)
    pos_ref[...] = pos.astype(jnp.int32)
    eidx_ref[...] = eidx
    total = cnt_s[...] + jnp.sum(ch, axis=0, keepdims=True)
    cnt_s[...] = total
    cnt_ref[...] = total


def _router(h, w_router_p, bias_p, *, tm=256):
    t, d = h.shape
    assert t % tm == 0
    tile = pl.BlockSpec((tm, LANES), lambda i: (i, 0))
    return pl.pallas_call(
        functools.partial(_router_kernel, tm=tm),
        grid=(t // tm,),
        in_specs=[pl.BlockSpec((tm, d), lambda i: (i, 0)),
                  pl.BlockSpec((d, LANES), lambda i: (0, 0)),
                  pl.BlockSpec((1, LANES), lambda i: (0, 0))],
        out_specs=[tile, tile, tile, pl.BlockSpec((SUBLANES, LANES), lambda i: (0, 0))],
        out_shape=[jax.ShapeDtypeStruct((t, LANES), jnp.int32),
                   jax.ShapeDtypeStruct((t, LANES), F32),
                   jax.ShapeDtypeStruct((t, LANES), jnp.int32),
                   jax.ShapeDtypeStruct((SUBLANES, LANES), F32)],
        scratch_shapes=[pltpu.VMEM((SUBLANES, LANES), F32)],
        compiler_params=_cparams(("arbitrary",)),
        name="router",
    )(h, w_router_p, bias_p)


def _gather_copy(h_hbm, xbuf, sem, tok, slot, r):
    return pltpu.make_async_copy(h_hbm.at[pl.ds(tok, 1), :], xbuf.at[slot, pl.ds(r, 1), :], sem.at[slot])


def _stream_expert_weights(i, n_used, be_ref, next_ref, w_hbms, stage, w16s, wsem, wslot):
    last = jnp.maximum(n_used - 1, 0)
    cur = be_ref[jnp.minimum(i, last)]
    prev = be_ref[jnp.minimum(jnp.maximum(i - 1, 0), last)]
    active = i < n_used
    arrays = range(len(w_hbms))

    def copy(a, e, s):
        return pltpu.make_async_copy(w_hbms[a].at[e], stage.at[s, a], wsem.at[s, a])

    @pl.when(active & (i == 0))
    def _():
        wslot[0] = 0
        for a in arrays:
            copy(a, cur, 0).start()

    @pl.when(active & ((i == 0) | (cur != prev)))
    def _():
        s = wslot[0]
        for a in arrays:
            copy(a, cur, s).wait()
        for a in arrays:
            w16s[a][...] = stage[s, a].astype(BF16)
        nxt = next_ref[cur]

        @pl.when(nxt >= 0)
        def _():
            for a in arrays:
                copy(a, nxt, 1 - s).start()
        wslot[0] = 1 - s


def _experts_up_kernel(be_ref, next_ref, tok_ref, nused_ref, h_hbm, wg_hbm, wu_hbm, hid_ref,
                       xbuf, stage, wg16, wu16, sem, wsem, wslot):
    i = pl.program_id(0)
    n_used = nused_ref[0]
    slot = lax.rem(i, 2)

    def wait_slot(s):
        def wbody(r, carry):
            _gather_copy(h_hbm, xbuf, sem, 0, s, r).wait()
            return carry
        lax.fori_loop(0, MOE_ROWS, wbody, 0, unroll=8)

    @pl.when((i == 0) & (n_used > 0))
    def _():
        def body(r, carry):
            _gather_copy(h_hbm, xbuf, sem, tok_ref[r], 0, r).start()
            return carry
        lax.fori_loop(0, MOE_ROWS, body, 0, unroll=8)

    _stream_expert_weights(i, n_used, be_ref, next_ref, (wg_hbm, wu_hbm), stage, (wg16, wu16), wsem, wslot)

    @pl.when(i < n_used)
    def _():
        base = jnp.minimum(i + 1, n_used - 1) * MOE_ROWS
        for r in range(MOE_ROWS):
            _gather_copy(h_hbm, xbuf, sem, tok_ref[base + r], 1 - slot, r).start()
        wait_slot(slot)
        hi, lo = _unpack_halves(xbuf[slot])
        x = jnp.concatenate([hi.astype(BF16), lo.astype(BF16)], axis=1)
        hid_ref[...] = (_silu(_dot(x, wg16[...])) * _dot(x, wu16[...])).astype(hid_ref.dtype)

    @pl.when(i == n_used - 1)
    def _():
        wait_slot(1 - slot)

    @pl.when(i >= n_used)
    def _():
        hid_ref[...] = jnp.zeros(hid_ref.shape, hid_ref.dtype)


def _experts_down_kernel(be_ref, next_ref, nused_ref, hid_ref, wd_hbm, y_ref, stage, wd16, wsem, wslot):
    i = pl.program_id(0)
    n_used = nused_ref[0]

    _stream_expert_weights(i, n_used, be_ref, next_ref, (wd_hbm,), stage, (wd16,), wsem, wslot)

    @pl.when(i < n_used)
    def _():
        y_ref[...] = _pack_halves(_dot(hid_ref[...], wd16[...]))

    @pl.when(i >= n_used)
    def _():
        y_ref[...] = jnp.zeros(y_ref.shape, y_ref.dtype)


def _experts(h, row_tok, block_e, next_e, n_used, wg, wu, wd):
    t, dh = h.shape
    d = 2 * dh
    n_blocks = block_e.shape[0]
    de = wg.shape[2]
    rows = n_blocks * MOE_ROWS
    hbm = pl.BlockSpec(memory_space=pl.ANY)

    hid = pl.pallas_call(
        _experts_up_kernel,
        grid_spec=pltpu.PrefetchScalarGridSpec(
            num_scalar_prefetch=4,
            grid=(n_blocks,),
            in_specs=[hbm, hbm, hbm],
            out_specs=pl.BlockSpec((MOE_ROWS, de), lambda i, *_: (i, 0)),
            scratch_shapes=[pltpu.VMEM((2, MOE_ROWS, dh), jnp.uint32),
                            pltpu.VMEM((2, 2, d, de), F32),
                            pltpu.VMEM((d, de), BF16), pltpu.VMEM((d, de), BF16),
                            pltpu.SemaphoreType.DMA((2,)), pltpu.SemaphoreType.DMA((2, 2)),
                            pltpu.SMEM((1,), jnp.int32)]),
        out_shape=jax.ShapeDtypeStruct((rows, de), BF16),
        compiler_params=_cparams(("arbitrary",)),
        name="experts_up",
    )(block_e, next_e, row_tok, n_used, h, wg, wu)
    return pl.pallas_call(
        _experts_down_kernel,
        grid_spec=pltpu.PrefetchScalarGridSpec(
            num_scalar_prefetch=3,
            grid=(n_blocks,),
            in_specs=[pl.BlockSpec((MOE_ROWS, de), lambda i, *_: (i, 0)), hbm],
            out_specs=pl.BlockSpec((MOE_ROWS, dh), lambda i, *_: (i, 0)),
            scratch_shapes=[pltpu.VMEM((2, 1, de, d), F32), pltpu.VMEM((de, d), BF16),
                            pltpu.SemaphoreType.DMA((2, 1)), pltpu.SMEM((1,), jnp.int32)]),
        out_shape=jax.ShapeDtypeStruct((rows, dh), jnp.uint32),
        compiler_params=_cparams(("arbitrary",)),
        name="experts_down",
    )(block_e, next_e, n_used, hid, wd)


def _shared_kernel(h_ref, wg_ref, wu_ref, wd_ref, o_ref):
    x = h_ref[...].astype(BF16)
    hid = _silu(_dot(x, wg_ref[...])) * _dot(x, wu_ref[...])
    o_ref[...] = _dot(hid.astype(BF16), wd_ref[...])


def _shared(h, wg, wu, wd, *, tm=512):
    t, d = h.shape
    ds_ = wg.shape[1]
    return pl.pallas_call(
        _shared_kernel,
        grid=(t // tm,),
        in_specs=[pl.BlockSpec((tm, d), lambda i: (i, 0)),
                  pl.BlockSpec((d, ds_), lambda i: (0, 0)),
                  pl.BlockSpec((d, ds_), lambda i: (0, 0)),
                  pl.BlockSpec((ds_, d), lambda i: (0, 0))],
        out_specs=pl.BlockSpec((tm, d), lambda i: (i, 0)),
        out_shape=jax.ShapeDtypeStruct((t, d), F32),
        compiler_params=_cparams(("parallel",)),
        name="shared_expert",
    )(h, wg, wu, wd)


def _combine_copy(y_hbm, ybuf, sem, row, slot, k, r):
    return pltpu.make_async_copy(y_hbm.at[pl.ds(row, 1), :], ybuf.at[slot, k, pl.ds(r, 1), :], sem.at[slot])


def _combine_kernel(dest_ref, y_hbm, h_ref, sh_ref, gate_ref, g_ref, b_ref, o_ref, ybuf, sem, *, tm):
    i = pl.program_id(0)
    n = pl.num_programs(0)
    slot = lax.rem(i, 2)

    def issue(tile, s):
        def body(r, carry):
            base = (tile * tm + r) * TOP_K
            for k in range(TOP_K):
                _combine_copy(y_hbm, ybuf, sem, dest_ref[base + k], s, k, r).start()
            return carry
        lax.fori_loop(0, tm, body, 0, unroll=4)

    @pl.when(i == 0)
    def _():
        issue(0, 0)

    @pl.when(i + 1 < n)
    def _():
        issue(i + 1, 1 - slot)

    def wbody(r, carry):
        for k in range(TOP_K):
            _combine_copy(y_hbm, ybuf, sem, 0, slot, k, r).wait()
        return carry
    lax.fori_loop(0, tm, wbody, 0)

    slab = 32
    dh = h_ref.shape[1] // 2
    for r in range(0, tm, slab):
        rs = slice(r, r + slab)
        gate = gate_ref[rs, :]
        res = DEEPNORM_ALPHA * h_ref[rs, :] + sh_ref[rs, :]
        acc_hi, acc_lo = res[:, :dh], res[:, dh:]
        for k in range(TOP_K):
            hi, lo = _unpack_halves(ybuf[slot, k, rs, :])
            acc_hi = acc_hi + gate[:, k:k + 1] * hi
            acc_lo = acc_lo + gate[:, k:k + 1] * lo
        o_ref[rs, :] = _layer_norm(jnp.concatenate([acc_hi, acc_lo], axis=1), g_ref[...], b_ref[...])


def _combine(dest_flat, y_rows, h, shared, gate, g, b, *, tm=128):
    t, d = h.shape
    assert t % tm == 0
    return pl.pallas_call(
        functools.partial(_combine_kernel, tm=tm),
        grid_spec=pltpu.PrefetchScalarGridSpec(
            num_scalar_prefetch=1,
            grid=(t // tm,),
            in_specs=[pl.BlockSpec(memory_space=pl.ANY),
                      pl.BlockSpec((tm, d), lambda i, dr: (i, 0)),
                      pl.BlockSpec((tm, d), lambda i, dr: (i, 0)),
                      pl.BlockSpec((tm, LANES), lambda i, dr: (i, 0)),
                      pl.BlockSpec((1, d), lambda i, dr: (0, 0)),
                      pl.BlockSpec((1, d), lambda i, dr: (0, 0))],
            out_specs=pl.BlockSpec((tm, d), lambda i, dr: (i, 0)),
            scratch_shapes=[pltpu.VMEM((2, TOP_K, tm, d // 2), jnp.uint32), pltpu.SemaphoreType.DMA((2,))]),
        out_shape=jax.ShapeDtypeStruct((t, d), F32),
        compiler_params=_cparams(("arbitrary",)),
        name="combine",
    )(dest_flat, y_rows, h, shared, gate, g, b)


def _layer(x, w_in, conv_w, a_log, dt_bias, norm_g, w_o_swa, w_o_gdn, w_out, ln1_g, ln1_b,
           w_router, router_bias, w_e_gate, w_e_up, w_e_down, w_s_gate, w_s_up, w_s_down, ln2_g, ln2_b):
    batch, seq, d = x.shape
    t = batch * seq
    n_groups = len(SWA_GROUPS)
    swa_w = n_groups * SWA_HEADS_PER_GROUP * HEAD_DIM
    gdn_w = GDN_HEADS * HEAD_DIM
    main_w = 3 * swa_w + 4 * gdn_w
    ba_w = 2 * GDN_HEADS
    assert w_in.shape[1] == main_w + ba_w + 2 * d

    xf = x.reshape(t, d)
    x16 = xf.astype(BF16)

    tn = 512
    gw = SWA_HEADS_PER_GROUP * HEAD_DIM
    assert gw % tn == 0 and (3 * swa_w) % tn == 0 and (4 * gdn_w) % tn == 0
    bpg = gw // tn

    w_in_t = w_in.T

    o_groups, lse_groups = [], []
    for j, (_, dil) in enumerate(SWA_GROUPS):
        col_map = lambda c, j=j: ((c // bpg) * n_groups + j) * bpg + c % bpg
        if dil == 1:
            qkv = _proj(x16, w_in_t, 3 * gw, tm=1024, tn=tn, out_dtype=BF16, col_map=col_map,
                        name=f"in_proj_swa{j}")
        else:
            qkv = _proj_residue(x16, w_in_t, 3 * gw, batch, seq, dil, tm=1024, tn=tn, out_dtype=BF16,
                                col_map=col_map, name=f"in_proj_swa{j}")
        o, lse = _swa_group(qkv, j, batch, seq)
        o_groups.append(o)
        lse_groups.append(lse)

    gdn_in = _proj(x16, w_in_t, 4 * gdn_w, tm=1024, tn=tn, out_dtype=BF16, name="in_proj_gdn",
                   col_map=lambda c: c + 3 * swa_w // tn)
    w_ba = jnp.pad(w_in_t[main_w:main_w + ba_w], ((0, LANES - ba_w), (0, 0)))
    ba = _proj(x16, w_ba, LANES, tm=1024, tn=LANES, out_dtype=F32, name="in_proj_ba")
    gates = _proj(x16, w_in_t[main_w + ba_w:], 2 * d, tm=1024, tn=tn, out_dtype=BF16, name="in_proj_gates")
    pair = 2 * GDN_CHUNK
    ba3 = ba[:, :ba_w].reshape(batch, seq // pair, pair, ba_w).transpose(0, 3, 1, 2)
    b_t, a_t = ba3[:, :GDN_HEADS], ba3[:, GDN_HEADS:]
    y_b = _gdn(gdn_in.reshape(batch, seq, 4 * gdn_w), conv_w.reshape(GDN_CONV, 3 * gdn_w), a_t, b_t,
               a_log.astype(F32), dt_bias.astype(F32), norm_g.reshape(1, HEAD_DIM).astype(F32), batch, seq, 0)
    y_b = y_b.reshape(t, gdn_w)

    merged = _merge(o_groups, lse_groups, y_b, gates, w_o_swa.astype(BF16), w_o_gdn.astype(BF16), d)
    h, h_packed = _proj_ln(merged, w_out.astype(BF16), xf, ln1_g.reshape(1, d), ln1_b.reshape(1, d))

    e_pad = LANES - N_EXPERTS
    eidx, gate, pos, cnt = _router(h, jnp.pad(w_router, ((0, 0), (0, e_pad))),
                                   jnp.pad(router_bias.reshape(1, N_EXPERTS), ((0, 0), (0, e_pad))))
    counts = cnt[0, :N_EXPERTS].astype(jnp.int32)
    padded = (counts + MOE_ROWS - 1) // MOE_ROWS * MOE_ROWS
    pend = jnp.cumsum(padded)
    pstart = pend - padded
    e_sel = eidx[:, :TOP_K, None] == jnp.arange(N_EXPERTS, dtype=jnp.int32)
    dest = jnp.sum(jnp.where(e_sel, pstart, 0), axis=-1) + pos[:, :TOP_K]
    n_blocks = (t * TOP_K + N_EXPERTS * (MOE_ROWS - 1) + MOE_ROWS - 1) // MOE_ROWS
    dest_flat = dest.reshape(-1)
    tok_of = jnp.arange(t * TOP_K, dtype=jnp.int32) // TOP_K
    row_tok = jnp.zeros((n_blocks * MOE_ROWS,), jnp.int32).at[dest_flat].set(tok_of)
    block_e = jnp.minimum(jnp.searchsorted(pend, jnp.arange(n_blocks, dtype=jnp.int32) * MOE_ROWS, side='right'),
                          N_EXPERTS - 1).astype(jnp.int32)
    n_used = (pend[-1:] // MOE_ROWS).astype(jnp.int32)
    e_ids = jnp.arange(N_EXPERTS, dtype=jnp.int32)
    first_owner = lax.cummin(jnp.where(padded > 0, e_ids, N_EXPERTS), reverse=True)
    next_e = jnp.concatenate([first_owner[1:], jnp.full((1,), N_EXPERTS, jnp.int32)])
    next_e = jnp.where(next_e >= N_EXPERTS, -1, next_e).astype(jnp.int32)
    y_rows = _experts(h_packed, row_tok, block_e, next_e, n_used, w_e_gate, w_e_up, w_e_down)
    shared = _shared(h, w_s_gate.astype(BF16), w_s_up.astype(BF16), w_s_down.astype(BF16))
    out = _combine(dest_flat, y_rows, h, shared, gate, ln2_g.reshape(1, d), ln2_b.reshape(1, d))
    return out.reshape(batch, seq, d)


@jax.jit
def kernel(x, w_in, conv_w, gdn_a_log, gdn_dt_bias, gdn_norm_g, w_o_swa, w_o_gdn, w_out, ln1_g, ln1_b,
           w_router, router_bias, w_e_gate, w_e_up, w_e_down, w_s_gate, w_s_up, w_s_down, ln2_g, ln2_b):
    for l in range(DEPTH):
        x = _layer(x, w_in[l], conv_w[l], gdn_a_log[l], gdn_dt_bias[l], gdn_norm_g[l], w_o_swa[l],
                   w_o_gdn[l], w_out[l], ln1_g[l], ln1_b[l], w_router[l], router_bias[l], w_e_gate[l],
                   w_e_up[l], w_e_down[l], w_s_gate[l], w_s_up[l], w_s_down[l], ln2_g[l], ln2_b[l])
    return x
```

```python
import functools
import math

import jax
import jax.numpy as jnp
from jax import lax
from jax.experimental import pallas as pl
from jax.experimental.pallas import tpu as pltpu

HEAD_DIM = 128
SWA_GROUPS = ((128, 1), (512, 4), (2048, 16))
SWA_HEADS_PER_GROUP = 8
GDN_HEADS = 16
GDN_CONV = 4
GDN_CHUNK = 64
N_EXPERTS = 96
TOP_K = 8
ROUTED_SCALE = 2.5
LN_EPS = 1e-5
RMS_EPS = 1e-6
DEPTH = 1
DEEPNORM_ALPHA = (2 * DEPTH) ** 0.25

LANES = 128
SUBLANES = 8
VMEM_LIMIT = 58 * 1024 * 1024
MOE_ROWS = 256

F32 = jnp.float32
BF16 = jnp.bfloat16
HIGHEST = lax.Precision.HIGHEST


def _cparams(sem):
    return pltpu.CompilerParams(dimension_semantics=sem, vmem_limit_bytes=VMEM_LIMIT)


def _dot(a, b):
    return jnp.dot(a, b, preferred_element_type=F32)


def _dot_nt(a, b):
    return lax.dot_general(a, b, (((1,), (1,)), ((), ())), preferred_element_type=F32)


def _dot_tn(a, b):
    return lax.dot_general(a, b, (((0,), (0,)), ((), ())), preferred_element_type=F32)


def _dot_hi(a, b):
    return jnp.dot(a, b, preferred_element_type=F32, precision=HIGHEST)


def _sigmoid(x):
    return 1.0 / (1.0 + jnp.exp(-x))


def _silu(x):
    return x * _sigmoid(x)


def _pack_halves(v):
    n = v.shape[1] // 2
    hi = pltpu.bitcast(v[:, :n].astype(BF16).astype(F32), jnp.uint32)
    lo = pltpu.bitcast(v[:, n:].astype(BF16).astype(F32), jnp.uint32)
    return hi | (lo >> 16)


def _unpack_halves(p):
    hi = pltpu.bitcast(p & jnp.uint32(0xFFFF0000), F32)
    lo = pltpu.bitcast(p << 16, F32)
    return hi, lo


def _proj_kernel(a_ref, bt_ref, o_ref):
    o_ref[...] = _dot_nt(a_ref[...].astype(BF16), bt_ref[...].astype(BF16)).astype(o_ref.dtype)


def _proj(a, bt, n_cols, *, tm, tn, out_dtype, col_map=lambda j: j, name="proj"):
    m, k = a.shape
    assert m % tm == 0 and n_cols % tn == 0 and bt.shape[1] == k
    return pl.pallas_call(
        _proj_kernel,
        grid=(m // tm, n_cols // tn),
        in_specs=[pl.BlockSpec((tm, k), lambda i, j: (i, 0)),
                  pl.BlockSpec((tn, k), lambda i, j: (col_map(j), 0))],
        out_specs=pl.BlockSpec((tm, tn), lambda i, j: (i, j)),
        out_shape=jax.ShapeDtypeStruct((m, n_cols), out_dtype),
        compiler_params=_cparams(("parallel", "arbitrary")),
        name=name,
    )(a, bt)


def _proj_residue_kernel(a_ref, bt_ref, o_ref, y_s, *, dil):
    y = _dot_nt(a_ref[...].astype(BF16), bt_ref[...].astype(BF16))
    rows = y_s.shape[1] // dil
    for c in range(y_s.shape[0]):
        cs = slice(c * LANES, (c + 1) * LANES)
        y_s[c] = y[:, cs]
        for r in range(dil):
            o_ref[r, :, cs] = y_s[c, pl.ds(r, rows, stride=dil), :].astype(o_ref.dtype)


def _proj_residue(a, bt, n_cols, batch, seq, dil, *, tm, tn, out_dtype, col_map, name):
    m, k = a.shape
    assert seq % tm == 0 and tm % (dil * 16) == 0 and n_cols % tn == 0 and bt.shape[1] == k
    spb = seq // tm
    out = pl.pallas_call(
        functools.partial(_proj_residue_kernel, dil=dil),
        grid=(m // tm, n_cols // tn),
        in_specs=[pl.BlockSpec((tm, k), lambda i, j: (i, 0)),
                  pl.BlockSpec((tn, k), lambda i, j: (col_map(j), 0))],
        out_specs=pl.BlockSpec((None, dil, tm // dil, tn), lambda i, j: (i // spb, 0, i % spb, j)),
        out_shape=jax.ShapeDtypeStruct((batch, dil, seq // dil, n_cols), out_dtype),
        scratch_shapes=[pltpu.VMEM((tn // LANES, tm, LANES), F32)],
        compiler_params=_cparams(("parallel", "arbitrary")),
        name=name,
    )(a, bt)
    return out.reshape(m, n_cols)


def _swa_kernel(q_ref, k_ref, v_ref, o_ref, lse_ref, *, n_heads, span, n_blocks):
    scale = HEAD_DIM ** -0.5
    qi = lax.broadcasted_iota(jnp.int32, (span, 2 * span), 0)
    ci = lax.broadcasted_iota(jnp.int32, (span, 2 * span), 1)
    mask_win = (ci >= qi) & (ci <= qi + span)
    qi0 = lax.broadcasted_iota(jnp.int32, (span, span), 0)
    ci0 = lax.broadcasted_iota(jnp.int32, (span, span), 1)
    mask_first = ci0 <= qi0
    lane = lax.broadcasted_iota(jnp.int32, (span, LANES), 1)

    def attend(q, kw, vw, mask):
        s = _dot_nt(q, kw) * scale
        s = jnp.where(mask, s, -jnp.inf)
        m = jnp.max(s, axis=-1, keepdims=True)
        p = jnp.exp(s - m)
        den = jnp.sum(p, axis=-1, keepdims=True)
        o = _dot(p.astype(BF16), vw) / den
        return o, m + jnp.log(den)

    def block(q0, k0, klen, mask):
        lse_acc = jnp.zeros((span, LANES), F32)
        for h in range(n_heads):
            cs = slice(h * HEAD_DIM, (h + 1) * HEAD_DIM)
            o, lse = attend(q_ref[pl.ds(q0, span), cs], k_ref[pl.ds(k0, klen), cs],
                            v_ref[pl.ds(k0, klen), cs], mask)
            o_ref[pl.ds(q0, span), cs] = o.astype(o_ref.dtype)
            lse_acc = jnp.where(lane == h, lse, lse_acc)
        lse_ref[pl.ds(q0, span), :] = lse_acc

    block(0, 0, span, mask_first)

    def body(n, carry):
        q0 = pl.multiple_of(n * span, span)
        k0 = pl.multiple_of((n - 1) * span, span)
        block(q0, k0, 2 * span, mask_win)
        return carry

    lax.fori_loop(1, n_blocks, body, 0)


def _swa_group(qkv, group, batch, seq):
    window, dil = SWA_GROUPS[group]
    span = window // dil
    sub_len = seq // dil
    assert seq % dil == 0 and sub_len % span == 0 and span % LANES == 0
    gw = SWA_HEADS_PER_GROUP * HEAD_DIM
    view = qkv.reshape(batch, dil, sub_len, 3 * gw)
    kern = functools.partial(_swa_kernel, n_heads=SWA_HEADS_PER_GROUP, span=span,
                             n_blocks=sub_len // span)

    def spec(seg):
        return pl.BlockSpec((None, None, sub_len, gw), lambda b, r: (b, r, 0, seg))

    o, lse = pl.pallas_call(
        kern,
        grid=(batch, dil),
        in_specs=[spec(0), spec(1), spec(2)],
        out_specs=[pl.BlockSpec((None, sub_len, gw), lambda b, r: (b, 0, r)),
                   pl.BlockSpec((None, sub_len, LANES), lambda b, r: (b, 0, r))],
        out_shape=[jax.ShapeDtypeStruct((batch, sub_len, dil * gw), BF16),
                   jax.ShapeDtypeStruct((batch, sub_len, dil * LANES), F32)],
        compiler_params=_cparams(("parallel", "parallel")),
        name=f"swa_g{group}",
    )(view, view, view)
    return o.reshape(batch * seq, gw), lse.reshape(batch * seq, LANES)


def _gdn_kernel(alog_ref, dtb_ref, q_ref, k_ref, v_ref, z_ref, cwq_ref, cwk_ref, cwv_ref,
                a_ref, b_ref, ng_ref, o_ref,
                xpad, qn, kn, vn, gcum_s, mq_s, op_s, n_s, gl_s, *, hb, seq):
    C = GDN_CHUNK
    P = 2 * C
    D = HEAD_DIM
    n_chunks = seq // C
    hg = pl.program_id(1)
    pad = SUBLANES

    xpad[0:pad, :] = jnp.zeros((pad, hb * D), F32)
    rows = 256
    for src_ref, cw_ref, dst, mode in ((q_ref, cwq_ref, qn, "q"), (k_ref, cwk_ref, kn, "k"),
                                       (v_ref, cwv_ref, vn, "v")):
        xpad[pad:pad + seq, :] = src_ref[...].astype(F32)
        cw = cw_ref[...]
        for t0 in range(0, seq, rows):
            acc = xpad[pad + t0:pad + t0 + rows, :] * cw[GDN_CONV - 1:GDN_CONV, :]
            for j in range(GDN_CONV - 1):
                sh = GDN_CONV - 1 - j
                acc = acc + xpad[pad + t0 - sh:pad + t0 - sh + rows, :] * cw[j:j + 1, :]
            y = _silu(acc)
            for hh in range(hb):
                cs = slice(hh * D, (hh + 1) * D)
                yh = y[:, cs]
                if mode != "v":
                    yh = yh * lax.rsqrt(jnp.sum(yh * yh, axis=-1, keepdims=True) + 1e-6)
                if mode == "q":
                    yh = yh * (D ** -0.5)
                dst[t0:t0 + rows, cs] = yh

    ri = lax.broadcasted_iota(jnp.int32, (P, P), 0)
    ci = lax.broadcasted_iota(jnp.int32, (P, P), 1)
    same = (ri // C) == (ci // C)
    tri = same & (ri >= ci)
    strict = same & (ri > ci)
    eye = ri == ci
    last = ci == (ri | (C - 1))
    cum_ones = (same & (ri <= ci)).astype(F32)
    eye_f = eye.astype(F32)
    first_chunk = lax.broadcasted_iota(jnp.int32, (P, D), 0) < C

    for hh in range(hb):
        head = hg * hb + hh
        sp_in = a_ref[hh] + dtb_ref[head]
        softplus = jnp.maximum(sp_in, 0.0) + jnp.log(1.0 + jnp.exp(-jnp.abs(sp_in)))
        gcum_s[hh] = _dot_hi(-jnp.exp(alog_ref[head]) * softplus, cum_ones)

    heads = range(hb)
    cols = [slice(hh * D, (hh + 1) * D) for hh in heads]

    pairs_per_step = 2
    items = [(g, hh) for g in range(pairs_per_step) for hh in heads]
    its = range(len(items))

    def pair_body(mb, carry):
        mp = [mb * pairs_per_step + g for g, _ in items]
        r0 = [pl.multiple_of(m * P, P) for m in mp]
        hd = [hh for _, hh in items]
        g_cum_r = [jnp.broadcast_to(gcum_s[hd[i], pl.ds(mp[i], 1), :], (P, P)) for i in its]
        beta_r = [jnp.broadcast_to(_sigmoid(b_ref[hd[i], pl.ds(mp[i], 1), :]), (P, P)) for i in its]
        q = [qn[pl.ds(r0[i], P), cols[hd[i]]] for i in its]
        k = [kn[pl.ds(r0[i], P), cols[hd[i]]] for i in its]
        v = [vn[pl.ds(r0[i], P), cols[hd[i]]] for i in its]
        g_cum_c = [jnp.sum(jnp.where(eye, g, 0.0), axis=1, keepdims=True) for g in g_cum_r]
        beta_c = [jnp.sum(jnp.where(eye, b, 0.0), axis=1, keepdims=True) for b in beta_r]
        g_last = [jnp.sum(jnp.where(last, g, 0.0), axis=1, keepdims=True) for g in g_cum_r]
        gam = [jnp.exp(jnp.where(tri, g_cum_c[i] - g_cum_r[i], -jnp.inf)) for i in its]
        kb = [k[i] * beta_c[i] for i in its]
        k16 = [x.astype(BF16) for x in k]
        aq = [_dot_nt(jnp.concatenate([kb[i].astype(BF16), q[i].astype(BF16)], axis=0), k16[i])
              for i in its]
        pw = [-jnp.where(strict, aq[i][:P] * gam[i], 0.0) for i in its]
        qk16 = [jnp.where(tri, aq[i][P:] * gam[i], 0.0).astype(BF16) for i in its]
        e_g = [jnp.exp(g) for g in g_cum_c]
        rhs16 = [jnp.concatenate([kb[i] * e_g[i], v[i] * beta_c[i]], axis=1).astype(BF16)
                 for i in its]
        kd = [k[i] * jnp.exp(g_last[i] - g_cum_c[i]) for i in its]
        t_inv = [eye_f + x for x in pw]
        p16 = [x.astype(BF16) for x in pw]
        pw = [_dot(x, x) for x in p16]
        for _ in range(int(math.log2(C)) - 2):
            p16 = [x.astype(BF16) for x in pw]
            tp = [_dot(jnp.concatenate([t_inv[i].astype(BF16), p16[i]], axis=0), p16[i]) for i in its]
            t_inv = [t_inv[i] + tp[i][:P] for i in its]
            pw = [tp[i][P:] for i in its]
        t_inv = [t_inv[i] + _dot(t_inv[i].astype(BF16), pw[i].astype(BF16)) for i in its]
        wu16 = [_dot(t_inv[i].astype(BF16), rhs16[i]).astype(BF16) for i in its]
        qo = [_dot(qk16[i], wu16[i]) for i in its]
        kd2 = [jnp.concatenate([jnp.where(first_chunk, x, 0.0), jnp.where(first_chunk, 0.0, x)],
                               axis=1).astype(BF16) for x in kd]
        mn = [_dot_tn(kd2[i], wu16[i]) for i in its]
        for i in its:
            hh = hd[i]
            op_s[pl.ds(r0[i], P), cols[hh]] = qo[i][:, D:]
            qp16 = (q[i] * e_g[i] - qo[i][:, :D]).astype(BF16)
            for c in range(P // C):
                chunk = mp[i] * (P // C) + c
                n0 = pl.multiple_of(chunk * D, D)
                q0 = pl.multiple_of(chunk * (D + C), D + C)
                g0 = pl.multiple_of(chunk * SUBLANES, SUBLANES)
                mq_s[hh, pl.ds(q0, D), :] = (-mn[i][c * D:(c + 1) * D, :D]).astype(BF16)
                mq_s[hh, pl.ds(q0 + D, C), :] = qp16[c * C:(c + 1) * C]
                n_s[hh, pl.ds(n0, D), :] = mn[i][c * D:(c + 1) * D, D:]
                gl_s[hh, pl.ds(g0, SUBLANES), :] = jnp.broadcast_to(
                    jnp.exp(g_last[i][c * C:c * C + SUBLANES, :]), (SUBLANES, D))
        return carry

    lax.fori_loop(0, seq // (P * pairs_per_step), pair_body, 0)

    ng = ng_ref[...]

    def rec_body(n, state):
        r0 = pl.multiple_of(n * C, C)
        n0 = pl.multiple_of(n * D, D)
        q0 = pl.multiple_of(n * (D + C), D + C)
        s16 = [s.astype(BF16) for s in state]
        ms = [_dot(mq_s[hh, pl.ds(q0, D + C), :], s16[hh]) for hh in heads]
        new_state = []
        for hh in heads:
            gl = gl_s[hh, pl.ds(pl.multiple_of(n * SUBLANES, SUBLANES), 1), :]
            new_state.append(state[hh] * gl + ms[hh][:D] + n_s[hh, pl.ds(n0, D), :])
        for hh in heads:
            cs = cols[hh]
            o = ms[hh][D:] + op_s[pl.ds(r0, C), cs]
            o = o * lax.rsqrt(jnp.mean(o * o, axis=-1, keepdims=True) + RMS_EPS) * ng
            z = z_ref[pl.ds(r0, C), cs].astype(F32)
            o_ref[pl.ds(r0, C), cs] = (o * _silu(z)).astype(o_ref.dtype)
        return tuple(new_state)

    lax.fori_loop(0, n_chunks, rec_body, tuple(jnp.zeros((D, D), F32) for _ in heads), unroll=2)


def _gdn(proj3, conv_w2, a_t, b_t, a_log, dt_bias, norm_g, batch, seq, col0, hb=2):
    C = GDN_CHUNK
    H = GDN_HEADS
    D = HEAD_DIM
    P = 2 * C
    bw = hb * D
    assert H % hb == 0 and col0 % bw == 0 and seq % 256 == 0 and P == LANES
    n_hg = H // hb
    c0 = col0 // bw
    nq = H * D // bw
    n_chunks = seq // C
    n_pairs = seq // P

    def pspec(seg):
        return pl.BlockSpec((None, seq, bw), lambda b, g, *_: (b, 0, c0 + seg * nq + g))

    def cwspec(seg):
        return pl.BlockSpec((GDN_CONV, bw), lambda b, g, *_: (0, seg * nq + g))

    abspec = pl.BlockSpec((None, hb, n_pairs, P), lambda b, g, *_: (b, g, 0, 0))
    kern = functools.partial(_gdn_kernel, hb=hb, seq=seq)
    return pl.pallas_call(
        kern,
        grid_spec=pltpu.PrefetchScalarGridSpec(
            num_scalar_prefetch=2,
            grid=(batch, n_hg),
            in_specs=[pspec(0), pspec(1), pspec(2), pspec(3), cwspec(0), cwspec(1), cwspec(2),
                      abspec, abspec, pl.BlockSpec((1, D), lambda b, g, *_: (0, 0))],
            out_specs=pl.BlockSpec((None, seq, bw), lambda b, g, *_: (b, 0, g)),
            scratch_shapes=[
                pltpu.VMEM((seq + SUBLANES, bw), F32),
                pltpu.VMEM((seq, bw), F32),
                pltpu.VMEM((seq, bw), F32),
                pltpu.VMEM((seq, bw), F32),
                pltpu.VMEM((hb, n_pairs, P), F32),
                pltpu.VMEM((hb, n_chunks * (D + C), D), BF16),
                pltpu.VMEM((seq, bw), F32),
                pltpu.VMEM((hb, n_chunks * D, D), F32),
                pltpu.VMEM((hb, n_chunks * SUBLANES, D), F32),
            ]),
        out_shape=jax.ShapeDtypeStruct((batch, seq, H * D), BF16),
        compiler_params=_cparams(("parallel", "parallel")),
        name="gdn",
    )(a_log, dt_bias, proj3, proj3, proj3, proj3, conv_w2, conv_w2, conv_w2, a_t, b_t, norm_g)


def _merge_kernel(o0_ref, o1_ref, o2_ref, l0_ref, l1_ref, l2_ref, yb_ref, ga_ref, gb_ref,
                  wa_ref, wb_ref, out_ref, ya_s):
    @pl.when(pl.program_id(1) == 0)
    def _():
        l0, l1, l2 = l0_ref[...], l1_ref[...], l2_ref[...]
        m = jnp.maximum(jnp.maximum(l0, l1), l2)
        e0, e1, e2 = jnp.exp(l0 - m), jnp.exp(l1 - m), jnp.exp(l2 - m)
        inv = 1.0 / (e0 + e1 + e2)
        w0, w1, w2 = e0 * inv, e1 * inv, e2 * inv
        for h in range(SWA_HEADS_PER_GROUP):
            cs = slice(h * HEAD_DIM, (h + 1) * HEAD_DIM)
            ya = (w0[:, h:h + 1] * o0_ref[:, cs].astype(F32) + w1[:, h:h + 1] * o1_ref[:, cs].astype(F32)
                  + w2[:, h:h + 1] * o2_ref[:, cs].astype(F32))
            ya_s[:, cs] = ya.astype(BF16)

    pa = _dot(ya_s[...], wa_ref[...])
    pb = _dot(yb_ref[...], wb_ref[...])
    out_ref[...] = (_sigmoid(ga_ref[...].astype(F32)) * pa
                    + _sigmoid(gb_ref[...].astype(F32)) * pb).astype(out_ref.dtype)


def _merge(o_groups, lse_groups, y_b, gates, w_o_swa, w_o_gdn, d_model, *, tm=512, tn=512):
    t = y_b.shape[0]
    wa_k, wb_k = w_o_swa.shape[0], w_o_gdn.shape[0]
    assert t % tm == 0 and d_model % tn == 0
    nb = d_model // tn
    row = lambda w: pl.BlockSpec((tm, w), lambda i, j: (i, 0))
    return pl.pallas_call(
        _merge_kernel,
        grid=(t // tm, nb),
        in_specs=[row(wa_k), row(wa_k), row(wa_k), row(LANES), row(LANES), row(LANES), row(wb_k),
                  pl.BlockSpec((tm, tn), lambda i, j: (i, j)),
                  pl.BlockSpec((tm, tn), lambda i, j: (i, nb + j)),
                  pl.BlockSpec((wa_k, tn), lambda i, j: (0, j)),
                  pl.BlockSpec((wb_k, tn), lambda i, j: (0, j))],
        out_specs=pl.BlockSpec((tm, tn), lambda i, j: (i, j)),
        out_shape=jax.ShapeDtypeStruct((t, d_model), BF16),
        scratch_shapes=[pltpu.VMEM((tm, wa_k), BF16)],
        compiler_params=_cparams(("parallel", "arbitrary")),
        name="merge",
    )(*o_groups, *lse_groups, y_b, gates, gates, w_o_swa, w_o_gdn)


def _layer_norm(v, g, b):
    mu = jnp.mean(v, axis=-1, keepdims=True)
    c = v - mu
    var = jnp.mean(c * c, axis=-1, keepdims=True)
    return c * lax.rsqrt(var + LN_EPS) * g + b


def _proj_ln_kernel(a_ref, w_ref, res_ref, g_ref, b_ref, o_ref, op_ref, *, tm):
    k = pl.program_id(1)
    part = _dot(a_ref[...], w_ref[...])

    @pl.when(k == 0)
    def _():
        o_ref[...] = part

    @pl.when(k > 0)
    def _():
        o_ref[...] += part

    @pl.when(k == pl.num_programs(1) - 1)
    def _():
        slab = 64
        for r in range(0, tm, slab):
            rs = slice(r, r + slab)
            hn = _layer_norm(DEEPNORM_ALPHA * res_ref[rs, :] + o_ref[rs, :], g_ref[...], b_ref[...])
            o_ref[rs, :] = hn
            packed = _pack_halves(hn)
            spt = packed.shape[1] // LANES
            for j in range(spt):
                op_ref[pl.ds(r * spt + j, slab, stride=spt), :] = packed[:, j * LANES:(j + 1) * LANES]


def _proj_ln(a, w, res, g, b, *, tm=512, tk=512):
    t, kdim = a.shape
    d = w.shape[1]
    assert t % tm == 0 and kdim % tk == 0
    return pl.pallas_call(
        functools.partial(_proj_ln_kernel, tm=tm),
        grid=(t // tm, kdim // tk),
        in_specs=[pl.BlockSpec((tm, tk), lambda i, k: (i, k)),
                  pl.BlockSpec((tk, d), lambda i, k: (k, 0)),
                  pl.BlockSpec((tm, d), lambda i, k: (i, 0), pipeline_mode=pl.Buffered(1)),
                  pl.BlockSpec((1, d), lambda i, k: (0, 0)),
                  pl.BlockSpec((1, d), lambda i, k: (0, 0))],
        out_specs=[pl.BlockSpec((tm, d), lambda i, k: (i, 0)),
                   pl.BlockSpec((tm * (d // 2 // LANES), LANES), lambda i, k: (i, 0))],
        out_shape=[jax.ShapeDtypeStruct((t, d), F32),
                   jax.ShapeDtypeStruct((t * (d // 2 // LANES), LANES), jnp.uint32)],
        compiler_params=_cparams(("parallel", "arbitrary")),
        name="proj_ln",
    )(a, w, res, g, b)


def _router_kernel(h_ref, w_ref, bias_ref, eidx_ref, gate_ref, pos_ref, cnt_ref, cnt_s, *, tm):
    i = pl.program_id(0)

    @pl.when(i == 0)
    def _():
        cnt_s[...] = jnp.zeros(cnt_s.shape, F32)

    scores = _sigmoid(_dot_hi(h_ref[...], w_ref[...]))
    lane = lax.broadcasted_iota(jnp.int32, (tm, LANES), 1)
    sel = jnp.where(lane < N_EXPERTS, scores + bias_ref[...], -jnp.inf)
    chosen = jnp.zeros((tm, LANES), jnp.bool_)
    eidx = jnp.zeros((tm, LANES), jnp.int32)
    gsel = jnp.zeros((tm, LANES), F32)
    picks = []
    for k in range(TOP_K):
        m = jnp.max(sel, axis=-1, keepdims=True)
        idx = jnp.min(jnp.where(sel == m, lane, LANES), axis=-1, keepdims=True)
        picks.append(idx)
        hit = lane == idx
        sc = jnp.sum(jnp.where(hit, scores, 0.0), axis=-1, keepdims=True)
        eidx = jnp.where(lane == k, idx, eidx)
        gsel = jnp.where(lane == k, sc, gsel)
        chosen = chosen | hit
        sel = jnp.where(hit, -jnp.inf, sel)
    gate_ref[...] = gsel / jnp.sum(gsel, axis=-1, keepdims=True) * ROUTED_SCALE

    ri = lax.broadcasted_iota(jnp.int32, (tm, tm), 0)
    ci = lax.broadcasted_iota(jnp.int32, (tm, tm), 1)
    ch = jnp.where(chosen, 1.0, 0.0)
    rank = _dot(jnp.where(ri > ci, 1.0, 0.0).astype(BF16), ch.astype(BF16)) + cnt_s[0:1, :]
    pos = jnp.zeros((tm, LANES), F32)
    for k in range(TOP_K):
        pk = jnp.sum(jnp.where(lane == picks[k], rank, 0.0), axis=-1, keepdims=True)
        pos = jnp.where(lane == k, pk, pos)
    pos_ref[...] = pos.astype(jnp.int32)
    eidx_ref[...] = eidx
    total = cnt_s[...] + jnp.sum(ch, axis=0, keepdims=True)
    cnt_s[...] = total
    cnt_ref[...] = total


def _router(h, w_router_p, bias_p, *, tm=256):
    t, d = h.shape
    assert t % tm == 0
    tile = pl.BlockSpec((tm, LANES), lambda i: (i, 0))
    return pl.pallas_call(
        functools.partial(_router_kernel, tm=tm),
        grid=(t // tm,),
        in_specs=[pl.BlockSpec((tm, d), lambda i: (i, 0)),
                  pl.BlockSpec((d, LANES), lambda i: (0, 0)),
                  pl.BlockSpec((1, LANES), lambda i: (0, 0))],
        out_specs=[tile, tile, tile, pl.BlockSpec((SUBLANES, LANES), lambda i: (0, 0))],
        out_shape=[jax.ShapeDtypeStruct((t, LANES), jnp.int32),
                   jax.ShapeDtypeStruct((t, LANES), F32),
                   jax.ShapeDtypeStruct((t, LANES), jnp.int32),
                   jax.ShapeDtypeStruct((SUBLANES, LANES), F32)],
        scratch_shapes=[pltpu.VMEM((SUBLANES, LANES), F32)],
        compiler_params=_cparams(("arbitrary",)),
        name="router",
    )(h, w_router_p, bias_p)


def _gather_copy(h_hbm, xbuf, sem, tok, slot, r):
    spt = h_hbm.shape[1]
    return pltpu.make_async_copy(h_hbm.at[tok], xbuf.at[slot, pl.ds(r * spt, spt), :], sem.at[slot])


def _stream_expert_weights(i, n_used, be_ref, next_ref, w_hbms, stage, w16s, wsem, wslot):
    last = jnp.maximum(n_used - 1, 0)
    cur = be_ref[jnp.minimum(i, last)]
    prev = be_ref[jnp.minimum(jnp.maximum(i - 1, 0), last)]
    active = i < n_used
    arrays = range(len(w_hbms))

    def copy(a, e, s):
        return pltpu.make_async_copy(w_hbms[a].at[e], stage.at[s, a], wsem.at[s, a])

    @pl.when(active & (i == 0))
    def _():
        wslot[0] = 0
        for a in arrays:
            copy(a, cur, 0).start()

    @pl.when(active & ((i == 0) | (cur != prev)))
    def _():
        s = wslot[0]
        for a in arrays:
            copy(a, cur, s).wait()
        rows, cols = stage.shape[2], stage.shape[3]
        step = min(rows, max(SUBLANES, 256 * SUBLANES * LANES // cols))
        for a in arrays:
            for r in range(0, rows, step):
                w16s[a][r:r + step, :] = stage[s, a, r:r + step, :].astype(BF16)
        nxt = next_ref[cur]

        @pl.when(nxt >= 0)
        def _():
            for a in arrays:
                copy(a, nxt, 1 - s).start()
        wslot[0] = 1 - s


def _experts_up_kernel(be_ref, next_ref, tok_ref, nused_ref, h_hbm, wg_hbm, wu_hbm, hid_ref,
                       xbuf, stage, wg16, wu16, sem, wsem, wslot):
    i = pl.program_id(0)
    n_used = nused_ref[0]
    slot = lax.rem(i, 2)

    def wait_slot(s):
        def wbody(r, carry):
            _gather_copy(h_hbm, xbuf, sem, 0, s, r).wait()
            return carry
        lax.fori_loop(0, MOE_ROWS, wbody, 0, unroll=8)

    @pl.when((i == 0) & (n_used > 0))
    def _():
        def body(r, carry):
            _gather_copy(h_hbm, xbuf, sem, tok_ref[r], 0, r).start()
            return carry
        lax.fori_loop(0, MOE_ROWS, body, 0, unroll=8)

    _stream_expert_weights(i, n_used, be_ref, next_ref, (wg_hbm, wu_hbm), stage, (wg16, wu16), wsem, wslot)

    @pl.when(i < n_used)
    def _():
        base = jnp.minimum(i + 1, n_used - 1) * MOE_ROWS
        for r in range(MOE_ROWS):
            _gather_copy(h_hbm, xbuf, sem, tok_ref[base + r], 1 - slot, r).start()
        wait_slot(slot)
        spt = h_hbm.shape[1]
        parts = [_unpack_halves(xbuf[slot, pl.ds(j, MOE_ROWS, stride=spt), :]) for j in range(spt)]
        x = jnp.concatenate([p[0].astype(BF16) for p in parts] + [p[1].astype(BF16) for p in parts], axis=1)
        hid_ref[...] = (_silu(_dot(x, wg16[...])) * _dot(x, wu16[...])).astype(hid_ref.dtype)

    @pl.when(i == n_used - 1)
    def _():
        wait_slot(1 - slot)

    @pl.when(i >= n_used)
    def _():
        hid_ref[...] = jnp.zeros(hid_ref.shape, hid_ref.dtype)


def _experts_down_kernel(be_ref, next_ref, nused_ref, hid_ref, wd_hbm, y_ref, stage, wd16, wsem, wslot):
    i = pl.program_id(0)
    n_used = nused_ref[0]
    spt = y_ref.shape[0] // MOE_ROWS

    _stream_expert_weights(i, n_used, be_ref, next_ref, (wd_hbm,), stage, (wd16,), wsem, wslot)

    @pl.when(i < n_used)
    def _():
        packed = _pack_halves(_dot(hid_ref[...], wd16[...]))
        for j in range(spt):
            y_ref[pl.ds(j, MOE_ROWS, stride=spt), :] = packed[:, j * LANES:(j + 1) * LANES]

    @pl.when(i >= n_used)
    def _():
        y_ref[...] = jnp.zeros(y_ref.shape, y_ref.dtype)


def _experts(h, row_tok, block_e, next_e, n_used, wg, wu, wd):
    d = wg.shape[1]
    dh = d // 2
    spt = dh // LANES
    h_slabs = h.reshape(-1, spt, LANES)
    n_blocks = block_e.shape[0]
    de = wg.shape[2]
    rows = n_blocks * MOE_ROWS
    hbm = pl.BlockSpec(memory_space=pl.ANY)

    hid = pl.pallas_call(
        _experts_up_kernel,
        grid_spec=pltpu.PrefetchScalarGridSpec(
            num_scalar_prefetch=4,
            grid=(n_blocks,),
            in_specs=[hbm, hbm, hbm],
            out_specs=pl.BlockSpec((MOE_ROWS, de), lambda i, *_: (i, 0)),
            scratch_shapes=[pltpu.VMEM((2, MOE_ROWS * spt, LANES), jnp.uint32),
                            pltpu.VMEM((2, 2, d, de), F32),
                            pltpu.VMEM((d, de), BF16), pltpu.VMEM((d, de), BF16),
                            pltpu.SemaphoreType.DMA((2,)), pltpu.SemaphoreType.DMA((2, 2)),
                            pltpu.SMEM((1,), jnp.int32)]),
        out_shape=jax.ShapeDtypeStruct((rows, de), BF16),
        compiler_params=_cparams(("arbitrary",)),
        name="experts_up",
    )(block_e, next_e, row_tok, n_used, h_slabs, wg, wu)
    return pl.pallas_call(
        _experts_down_kernel,
        grid_spec=pltpu.PrefetchScalarGridSpec(
            num_scalar_prefetch=3,
            grid=(n_blocks,),
            in_specs=[pl.BlockSpec((MOE_ROWS, de), lambda i, *_: (i, 0)), hbm],
            out_specs=pl.BlockSpec((MOE_ROWS * spt, LANES), lambda i, *_: (i, 0)),
            scratch_shapes=[pltpu.VMEM((2, 1, de, d), F32), pltpu.VMEM((de, d), BF16),
                            pltpu.SemaphoreType.DMA((2, 1)), pltpu.SMEM((1,), jnp.int32)]),
        out_shape=jax.ShapeDtypeStruct((rows * spt, LANES), jnp.uint32),
        compiler_params=_cparams(("arbitrary",)),
        name="experts_down",
    )(block_e, next_e, n_used, hid, wd)


def _shared_kernel(h_ref, wg_ref, wu_ref, wd_ref, o_ref):
    x = h_ref[...].astype(BF16)
    hid = _silu(_dot(x, wg_ref[...])) * _dot(x, wu_ref[...])
    o_ref[...] = _dot(hid.astype(BF16), wd_ref[...])


def _shared(h, wg, wu, wd, *, tm=512):
    t, d = h.shape
    ds_ = wg.shape[1]
    return pl.pallas_call(
        _shared_kernel,
        grid=(t // tm,),
        in_specs=[pl.BlockSpec((tm, d), lambda i: (i, 0)),
                  pl.BlockSpec((d, ds_), lambda i: (0, 0)),
                  pl.BlockSpec((d, ds_), lambda i: (0, 0)),
                  pl.BlockSpec((ds_, d), lambda i: (0, 0))],
        out_specs=pl.BlockSpec((tm, d), lambda i: (i, 0)),
        out_shape=jax.ShapeDtypeStruct((t, d), F32),
        compiler_params=_cparams(("parallel",)),
        name="shared_expert",
    )(h, wg, wu, wd)


def _combine_copy(y_hbm, ybuf, sem, row, slot, k, r):
    spt = y_hbm.shape[1]
    return pltpu.make_async_copy(y_hbm.at[row], ybuf.at[slot, k, pl.ds(r * spt, spt), :], sem.at[slot])


def _combine_kernel(dest_ref, y_hbm, h_ref, sh_ref, gate_ref, g_ref, b_ref, o_ref, ybuf, sem, *, tm):
    i = pl.program_id(0)
    n = pl.num_programs(0)
    slot = lax.rem(i, 2)

    def issue(tile, s):
        def body(r, carry):
            base = (tile * tm + r) * TOP_K
            for k in range(TOP_K):
                _combine_copy(y_hbm, ybuf, sem, dest_ref[base + k], s, k, r).start()
            return carry
        lax.fori_loop(0, tm, body, 0, unroll=4)

    @pl.when(i == 0)
    def _():
        issue(0, 0)

    @pl.when(i + 1 < n)
    def _():
        issue(i + 1, 1 - slot)

    def wbody(r, carry):
        for k in range(TOP_K):
            _combine_copy(y_hbm, ybuf, sem, 0, slot, k, r).wait()
        return carry
    lax.fori_loop(0, tm, wbody, 0)

    slab = 32
    dh = h_ref.shape[1] // 2
    spt = y_hbm.shape[1]
    for r in range(0, tm, slab):
        rs = slice(r, r + slab)
        gate = gate_ref[rs, :]
        gk = [gate[:, k:k + 1] for k in range(TOP_K)]
        res = DEEPNORM_ALPHA * h_ref[rs, :] + sh_ref[rs, :]
        his, los = [], []
        for j in range(spt):
            acc_hi = res[:, j * LANES:(j + 1) * LANES]
            acc_lo = res[:, dh + j * LANES:dh + (j + 1) * LANES]
            for k in range(TOP_K):
                hi, lo = _unpack_halves(ybuf[slot, k, pl.ds(r * spt + j, slab, stride=spt), :])
                acc_hi = acc_hi + gk[k] * hi
                acc_lo = acc_lo + gk[k] * lo
            his.append(acc_hi)
            los.append(acc_lo)
        o_ref[rs, :] = _layer_norm(jnp.concatenate(his + los, axis=1), g_ref[...], b_ref[...])


def _combine(dest_flat, y_rows, h, shared, gate, g, b, *, tm=128):
    t, d = h.shape
    spt = d // 2 // LANES
    assert t % tm == 0
    y_rows = y_rows.reshape(-1, spt, LANES)
    return pl.pallas_call(
        functools.partial(_combine_kernel, tm=tm),
        grid_spec=pltpu.PrefetchScalarGridSpec(
            num_scalar_prefetch=1,
            grid=(t // tm,),
            in_specs=[pl.BlockSpec(memory_space=pl.ANY),
                      pl.BlockSpec((tm, d), lambda i, dr: (i, 0)),
                      pl.BlockSpec((tm, d), lambda i, dr: (i, 0)),
                      pl.BlockSpec((tm, LANES), lambda i, dr: (i, 0)),
                      pl.BlockSpec((1, d), lambda i, dr: (0, 0)),
                      pl.BlockSpec((1, d), lambda i, dr: (0, 0))],
            out_specs=pl.BlockSpec((tm, d), lambda i, dr: (i, 0)),
            scratch_shapes=[pltpu.VMEM((2, TOP_K, tm * spt, LANES), jnp.uint32),
                            pltpu.SemaphoreType.DMA((2,))]),
        out_shape=jax.ShapeDtypeStruct((t, d), F32),
        compiler_params=_cparams(("arbitrary",)),
        name="combine",
    )(dest_flat, y_rows, h, shared, gate, g, b)


def _layer(x, w_in, conv_w, a_log, dt_bias, norm_g, w_o_swa, w_o_gdn, w_out, ln1_g, ln1_b,
           w_router, router_bias, w_e_gate, w_e_up, w_e_down, w_s_gate, w_s_up, w_s_down, ln2_g, ln2_b):
    batch, seq, d = x.shape
    t = batch * seq
    n_groups = len(SWA_GROUPS)
    swa_w = n_groups * SWA_HEADS_PER_GROUP * HEAD_DIM
    gdn_w = GDN_HEADS * HEAD_DIM
    main_w = 3 * swa_w + 4 * gdn_w
    ba_w = 2 * GDN_HEADS
    assert w_in.shape[1] == main_w + ba_w + 2 * d

    xf = x.reshape(t, d)
    x16 = xf.astype(BF16)

    tn = 512
    gw = SWA_HEADS_PER_GROUP * HEAD_DIM
    assert gw % tn == 0 and (3 * swa_w) % tn == 0 and (4 * gdn_w) % tn == 0
    bpg = gw // tn

    w_in_t = w_in.T

    o_groups, lse_groups = [], []
    for j, (_, dil) in enumerate(SWA_GROUPS):
        col_map = lambda c, j=j: ((c // bpg) * n_groups + j) * bpg + c % bpg
        if dil == 1:
            qkv = _proj(x16, w_in_t, 3 * gw, tm=1024, tn=tn, out_dtype=BF16, col_map=col_map,
                        name=f"in_proj_swa{j}")
        else:
            qkv = _proj_residue(x16, w_in_t, 3 * gw, batch, seq, dil, tm=1024, tn=tn, out_dtype=BF16,
                                col_map=col_map, name=f"in_proj_swa{j}")
        o, lse = _swa_group(qkv, j, batch, seq)
        o_groups.append(o)
        lse_groups.append(lse)

    gdn_in = _proj(x16, w_in_t, 4 * gdn_w, tm=1024, tn=tn, out_dtype=BF16, name="in_proj_gdn",
                   col_map=lambda c: c + 3 * swa_w // tn)
    w_ba = jnp.pad(w_in_t[main_w:main_w + ba_w], ((0, LANES - ba_w), (0, 0)))
    ba = _proj(x16, w_ba, LANES, tm=1024, tn=LANES, out_dtype=F32, name="in_proj_ba")
    gates = _proj(x16, w_in_t[main_w + ba_w:], 2 * d, tm=1024, tn=tn, out_dtype=BF16, name="in_proj_gates")
    pair = 2 * GDN_CHUNK
    ba3 = ba[:, :ba_w].reshape(batch, seq // pair, pair, ba_w).transpose(0, 3, 1, 2)
    b_t, a_t = ba3[:, :GDN_HEADS], ba3[:, GDN_HEADS:]
    y_b = _gdn(gdn_in.reshape(batch, seq, 4 * gdn_w), conv_w.reshape(GDN_CONV, 3 * gdn_w), a_t, b_t,
               a_log.astype(F32), dt_bias.astype(F32), norm_g.reshape(1, HEAD_DIM).astype(F32), batch, seq, 0)
    y_b = y_b.reshape(t, gdn_w)

    merged = _merge(o_groups, lse_groups, y_b, gates, w_o_swa.astype(BF16), w_o_gdn.astype(BF16), d)
    h, h_packed = _proj_ln(merged, w_out.astype(BF16), xf, ln1_g.reshape(1, d), ln1_b.reshape(1, d))

    e_pad = LANES - N_EXPERTS
    eidx, gate, pos, cnt = _router(h, jnp.pad(w_router, ((0, 0), (0, e_pad))),
                                   jnp.pad(router_bias.reshape(1, N_EXPERTS), ((0, 0), (0, e_pad))))
    counts = cnt[0, :N_EXPERTS].astype(jnp.int32)
    padded = (counts + MOE_ROWS - 1) // MOE_ROWS * MOE_ROWS
    pend = jnp.cumsum(padded)
    pstart = pend - padded
    e_sel = eidx[:, :TOP_K, None] == jnp.arange(N_EXPERTS, dtype=jnp.int32)
    dest = jnp.sum(jnp.where(e_sel, pstart, 0), axis=-1) + pos[:, :TOP_K]
    n_blocks = (t * TOP_K + N_EXPERTS * (MOE_ROWS - 1) + MOE_ROWS - 1) // MOE_ROWS
    dest_flat = dest.reshape(-1)
    tok_of = jnp.arange(t * TOP_K, dtype=jnp.int32) // TOP_K
    row_tok = jnp.zeros((n_blocks * MOE_ROWS,), jnp.int32).at[dest_flat].set(tok_of)
    block_e = jnp.minimum(jnp.searchsorted(pend, jnp.arange(n_blocks, dtype=jnp.int32) * MOE_ROWS, side='right'),
                          N_EXPERTS - 1).astype(jnp.int32)
    n_used = (pend[-1:] // MOE_ROWS).astype(jnp.int32)
    e_ids = jnp.arange(N_EXPERTS, dtype=jnp.int32)
    first_owner = lax.cummin(jnp.where(padded > 0, e_ids, N_EXPERTS), reverse=True)
    next_e = jnp.concatenate([first_owner[1:], jnp.full((1,), N_EXPERTS, jnp.int32)])
    next_e = jnp.where(next_e >= N_EXPERTS, -1, next_e).astype(jnp.int32)
    y_rows = _experts(h_packed, row_tok, block_e, next_e, n_used, w_e_gate, w_e_up, w_e_down)
    shared = _shared(h, w_s_gate.astype(BF16), w_s_up.astype(BF16), w_s_down.astype(BF16))
    out = _combine(dest_flat, y_rows, h, shared, gate, ln2_g.reshape(1, d), ln2_b.reshape(1, d))
    return out.reshape(batch, seq, d)


@jax.jit
def kernel(x, w_in, conv_w, gdn_a_log, gdn_dt_bias, gdn_norm_g, w_o_swa, w_o_gdn, w_out, ln1_g, ln1_b,
           w_router, router_bias, w_e_gate, w_e_up, w_e_down, w_s_gate, w_s_up, w_s_down, ln2_g, ln2_b):
    for l in range(DEPTH):
        x = _layer(x, w_in[l], conv_w[l], gdn_a_log[l], gdn_dt_bias[l], gdn_norm_g[l], w_o_swa[l],
                   w_o_gdn[l], w_out[l], ln1_g[l], ln1_b[l], w_router[l], router_bias[l], w_e_gate[l],
                   w_e_up[l], w_e_down[l], w_s_gate[l], w_s_up[l], w_s_down[l], ln2_g[l], ln2_b[l])
    return x
```

```python
import functools
import math

import jax
import jax.numpy as jnp
from jax import lax
from jax.experimental import pallas as pl
from jax.experimental.pallas import tpu as pltpu

HEAD_DIM = 128
SWA_GROUPS = ((128, 1), (512, 4), (2048, 16))
SWA_HEADS_PER_GROUP = 8
GDN_HEADS = 16
GDN_CONV = 4
GDN_CHUNK = 64
N_EXPERTS = 96
TOP_K = 8
ROUTED_SCALE = 2.5
LN_EPS = 1e-5
RMS_EPS = 1e-6
DEPTH = 1
DEEPNORM_ALPHA = (2 * DEPTH) ** 0.25

LANES = 128
SUBLANES = 8
VMEM_LIMIT = 58 * 1024 * 1024
MOE_ROWS = 256
WEIGHT_COPIES = 8

F32 = jnp.float32
BF16 = jnp.bfloat16
HIGHEST = lax.Precision.HIGHEST


def _cparams(sem):
    return pltpu.CompilerParams(dimension_semantics=sem, vmem_limit_bytes=VMEM_LIMIT)


def _dot(a, b):
    return jnp.dot(a, b, preferred_element_type=F32)


def _dot_nt(a, b):
    return lax.dot_general(a, b, (((1,), (1,)), ((), ())), preferred_element_type=F32)


def _dot_tn(a, b):
    return lax.dot_general(a, b, (((0,), (0,)), ((), ())), preferred_element_type=F32)


def _dot_hi(a, b):
    return jnp.dot(a, b, preferred_element_type=F32, precision=HIGHEST)


def _sigmoid(x):
    return 1.0 / (1.0 + jnp.exp(-x))


def _silu(x):
    return x * _sigmoid(x)


def _pack_halves(v):
    n = v.shape[1] // 2
    hi = pltpu.bitcast(v[:, :n].astype(BF16).astype(F32), jnp.uint32)
    lo = pltpu.bitcast(v[:, n:].astype(BF16).astype(F32), jnp.uint32)
    return hi | (lo >> 16)


def _unpack_halves(p):
    hi = pltpu.bitcast(p & jnp.uint32(0xFFFF0000), F32)
    lo = pltpu.bitcast(p << 16, F32)
    return hi, lo


def _proj_kernel(a_ref, bt_ref, o_ref):
    o_ref[...] = _dot_nt(a_ref[...].astype(BF16), bt_ref[...].astype(BF16)).astype(o_ref.dtype)


def _proj(a, bt, n_cols, *, tm, tn, out_dtype, col_map=lambda j: j, name="proj"):
    m, k = a.shape
    assert m % tm == 0 and n_cols % tn == 0 and bt.shape[1] == k
    return pl.pallas_call(
        _proj_kernel,
        grid=(m // tm, n_cols // tn),
        in_specs=[pl.BlockSpec((tm, k), lambda i, j: (i, 0)),
                  pl.BlockSpec((tn, k), lambda i, j: (col_map(j), 0))],
        out_specs=pl.BlockSpec((tm, tn), lambda i, j: (i, j)),
        out_shape=jax.ShapeDtypeStruct((m, n_cols), out_dtype),
        compiler_params=_cparams(("parallel", "arbitrary")),
        name=name,
    )(a, bt)


def _proj_residue_kernel(a_ref, bt_ref, o_ref, y_s, *, dil):
    y = _dot_nt(a_ref[...].astype(BF16), bt_ref[...].astype(BF16))
    rows = y_s.shape[1] // dil
    for c in range(y_s.shape[0]):
        cs = slice(c * LANES, (c + 1) * LANES)
        y_s[c] = y[:, cs]
        for r in range(dil):
            o_ref[r, :, cs] = y_s[c, pl.ds(r, rows, stride=dil), :].astype(o_ref.dtype)


def _proj_residue(a, bt, n_cols, batch, seq, dil, *, tm, tn, out_dtype, col_map, name):
    m, k = a.shape
    assert seq % tm == 0 and tm % (dil * 16) == 0 and n_cols % tn == 0 and bt.shape[1] == k
    spb = seq // tm
    out = pl.pallas_call(
        functools.partial(_proj_residue_kernel, dil=dil),
        grid=(m // tm, n_cols // tn),
        in_specs=[pl.BlockSpec((tm, k), lambda i, j: (i, 0)),
                  pl.BlockSpec((tn, k), lambda i, j: (col_map(j), 0))],
        out_specs=pl.BlockSpec((None, dil, tm // dil, tn), lambda i, j: (i // spb, 0, i % spb, j)),
        out_shape=jax.ShapeDtypeStruct((batch, dil, seq // dil, n_cols), out_dtype),
        scratch_shapes=[pltpu.VMEM((tn // LANES, tm, LANES), F32)],
        compiler_params=_cparams(("parallel", "arbitrary")),
        name=name,
    )(a, bt)
    return out.reshape(m, n_cols)


def _swa_kernel(q_ref, k_ref, v_ref, o_ref, lse_ref, *, n_heads, span, n_blocks):
    scale = HEAD_DIM ** -0.5
    qi = lax.broadcasted_iota(jnp.int32, (span, 2 * span), 0)
    ci = lax.broadcasted_iota(jnp.int32, (span, 2 * span), 1)
    mask_win = (ci >= qi) & (ci <= qi + span)
    qi0 = lax.broadcasted_iota(jnp.int32, (span, span), 0)
    ci0 = lax.broadcasted_iota(jnp.int32, (span, span), 1)
    mask_first = ci0 <= qi0
    lane = lax.broadcasted_iota(jnp.int32, (span, LANES), 1)

    def attend(q, kw, vw, mask):
        s = _dot_nt(q, kw) * scale
        s = jnp.where(mask, s, -jnp.inf)
        m = jnp.max(s, axis=-1, keepdims=True)
        p = jnp.exp(s - m)
        den = jnp.sum(p, axis=-1, keepdims=True)
        o = _dot(p.astype(BF16), vw) / den
        return o, m + jnp.log(den)

    def block(q0, k0, klen, mask):
        lse_acc = jnp.zeros((span, LANES), F32)
        for h in range(n_heads):
            cs = slice(h * HEAD_DIM, (h + 1) * HEAD_DIM)
            o, lse = attend(q_ref[pl.ds(q0, span), cs], k_ref[pl.ds(k0, klen), cs],
                            v_ref[pl.ds(k0, klen), cs], mask)
            o_ref[pl.ds(q0, span), cs] = o.astype(o_ref.dtype)
            lse_acc = jnp.where(lane == h, lse, lse_acc)
        lse_ref[pl.ds(q0, span), :] = lse_acc

    block(0, 0, span, mask_first)

    def body(n, carry):
        q0 = pl.multiple_of(n * span, span)
        k0 = pl.multiple_of((n - 1) * span, span)
        block(q0, k0, 2 * span, mask_win)
        return carry

    lax.fori_loop(1, n_blocks, body, 0)


def _swa_group(qkv, group, batch, seq):
    window, dil = SWA_GROUPS[group]
    span = window // dil
    sub_len = seq // dil
    assert seq % dil == 0 and sub_len % span == 0 and span % LANES == 0
    gw = SWA_HEADS_PER_GROUP * HEAD_DIM
    view = qkv.reshape(batch, dil, sub_len, 3 * gw)
    kern = functools.partial(_swa_kernel, n_heads=SWA_HEADS_PER_GROUP, span=span,
                             n_blocks=sub_len // span)

    def spec(seg):
        return pl.BlockSpec((None, None, sub_len, gw), lambda b, r: (b, r, 0, seg))

    o, lse = pl.pallas_call(
        kern,
        grid=(batch, dil),
        in_specs=[spec(0), spec(1), spec(2)],
        out_specs=[pl.BlockSpec((None, sub_len, gw), lambda b, r: (b, 0, r)),
                   pl.BlockSpec((None, sub_len, LANES), lambda b, r: (b, 0, r))],
        out_shape=[jax.ShapeDtypeStruct((batch, sub_len, dil * gw), BF16),
                   jax.ShapeDtypeStruct((batch, sub_len, dil * LANES), F32)],
        compiler_params=_cparams(("parallel", "parallel")),
        name=f"swa_g{group}",
    )(view, view, view)
    return o.reshape(batch * seq, gw), lse.reshape(batch * seq, LANES)


def _gdn_kernel(alog_ref, dtb_ref, q_ref, k_ref, v_ref, z_ref, cwq_ref, cwk_ref, cwv_ref,
                a_ref, b_ref, ng_ref, o_ref,
                xpad, qn, kn, vn, gcum_s, mq_s, op_s, n_s, gl_s, *, hb, seq):
    C = GDN_CHUNK
    P = 2 * C
    D = HEAD_DIM
    n_chunks = seq // C
    hg = pl.program_id(1)
    pad = SUBLANES

    xpad[0:pad, :] = jnp.zeros((pad, hb * D), F32)
    rows = 256
    for src_ref, cw_ref, dst, mode in ((q_ref, cwq_ref, qn, "q"), (k_ref, cwk_ref, kn, "k"),
                                       (v_ref, cwv_ref, vn, "v")):
        xpad[pad:pad + seq, :] = src_ref[...].astype(F32)
        cw = cw_ref[...]
        for t0 in range(0, seq, rows):
            acc = xpad[pad + t0:pad + t0 + rows, :] * cw[GDN_CONV - 1:GDN_CONV, :]
            for j in range(GDN_CONV - 1):
                sh = GDN_CONV - 1 - j
                acc = acc + xpad[pad + t0 - sh:pad + t0 - sh + rows, :] * cw[j:j + 1, :]
            y = _silu(acc)
            for hh in range(hb):
                cs = slice(hh * D, (hh + 1) * D)
                yh = y[:, cs]
                if mode != "v":
                    yh = yh * lax.rsqrt(jnp.sum(yh * yh, axis=-1, keepdims=True) + 1e-6)
                if mode == "q":
                    yh = yh * (D ** -0.5)
                dst[t0:t0 + rows, cs] = yh

    ri = lax.broadcasted_iota(jnp.int32, (P, P), 0)
    ci = lax.broadcasted_iota(jnp.int32, (P, P), 1)
    same = (ri // C) == (ci // C)
    tri = same & (ri >= ci)
    strict = same & (ri > ci)
    eye = ri == ci
    last = ci == (ri | (C - 1))
    cum_ones = (same & (ri <= ci)).astype(F32)
    eye_f = eye.astype(F32)
    first_chunk = lax.broadcasted_iota(jnp.int32, (P, D), 0) < C

    for hh in range(hb):
        head = hg * hb + hh
        sp_in = a_ref[hh] + dtb_ref[head]
        softplus = jnp.maximum(sp_in, 0.0) + jnp.log(1.0 + jnp.exp(-jnp.abs(sp_in)))
        gcum_s[hh] = _dot_hi(-jnp.exp(alog_ref[head]) * softplus, cum_ones)

    heads = range(hb)
    cols = [slice(hh * D, (hh + 1) * D) for hh in heads]

    pairs_per_step = 2
    items = [(g, hh) for g in range(pairs_per_step) for hh in heads]
    its = range(len(items))

    def pair_body(mb, carry):
        mp = [mb * pairs_per_step + g for g, _ in items]
        r0 = [pl.multiple_of(m * P, P) for m in mp]
        hd = [hh for _, hh in items]
        g_cum_r = [jnp.broadcast_to(gcum_s[hd[i], pl.ds(mp[i], 1), :], (P, P)) for i in its]
        beta_r = [jnp.broadcast_to(_sigmoid(b_ref[hd[i], pl.ds(mp[i], 1), :]), (P, P)) for i in its]
        q = [qn[pl.ds(r0[i], P), cols[hd[i]]] for i in its]
        k = [kn[pl.ds(r0[i], P), cols[hd[i]]] for i in its]
        v = [vn[pl.ds(r0[i], P), cols[hd[i]]] for i in its]
        g_cum_c = [jnp.sum(jnp.where(eye, g, 0.0), axis=1, keepdims=True) for g in g_cum_r]
        beta_c = [jnp.sum(jnp.where(eye, b, 0.0), axis=1, keepdims=True) for b in beta_r]
        g_last = [jnp.sum(jnp.where(last, g, 0.0), axis=1, keepdims=True) for g in g_cum_r]
        gam = [jnp.exp(jnp.where(tri, g_cum_c[i] - g_cum_r[i], -jnp.inf)) for i in its]
        kb = [k[i] * beta_c[i] for i in its]
        k16 = [x.astype(BF16) for x in k]
        aq = [_dot_nt(jnp.concatenate([kb[i].astype(BF16), q[i].astype(BF16)], axis=0), k16[i])
              for i in its]
        pw = [-jnp.where(strict, aq[i][:P] * gam[i], 0.0) for i in its]
        qk16 = [jnp.where(tri, aq[i][P:] * gam[i], 0.0).astype(BF16) for i in its]
        e_g = [jnp.exp(g) for g in g_cum_c]
        rhs16 = [jnp.concatenate([kb[i] * e_g[i], v[i] * beta_c[i]], axis=1).astype(BF16)
                 for i in its]
        kd = [k[i] * jnp.exp(g_last[i] - g_cum_c[i]) for i in its]
        t_inv = [eye_f + x for x in pw]
        p16 = [x.astype(BF16) for x in pw]
        pw = [_dot(x, x) for x in p16]
        for _ in range(int(math.log2(C)) - 2):
            p16 = [x.astype(BF16) for x in pw]
            tp = [_dot(jnp.concatenate([t_inv[i].astype(BF16), p16[i]], axis=0), p16[i]) for i in its]
            t_inv = [t_inv[i] + tp[i][:P] for i in its]
            pw = [tp[i][P:] for i in its]
        t_inv = [t_inv[i] + _dot(t_inv[i].astype(BF16), pw[i].astype(BF16)) for i in its]
        wu16 = [_dot(t_inv[i].astype(BF16), rhs16[i]).astype(BF16) for i in its]
        qo = [_dot(qk16[i], wu16[i]) for i in its]
        kd2 = [jnp.concatenate([jnp.where(first_chunk, x, 0.0), jnp.where(first_chunk, 0.0, x)],
                               axis=1).astype(BF16) for x in kd]
        mn = [_dot_tn(kd2[i], wu16[i]) for i in its]
        for i in its:
            hh = hd[i]
            op_s[pl.ds(r0[i], P), cols[hh]] = qo[i][:, D:]
            qp16 = (q[i] * e_g[i] - qo[i][:, :D]).astype(BF16)
            for c in range(P // C):
                chunk = mp[i] * (P // C) + c
                n0 = pl.multiple_of(chunk * D, D)
                q0 = pl.multiple_of(chunk * (D + C), D + C)
                g0 = pl.multiple_of(chunk * SUBLANES, SUBLANES)
                mq_s[hh, pl.ds(q0, D), :] = (-mn[i][c * D:(c + 1) * D, :D]).astype(BF16)
                mq_s[hh, pl.ds(q0 + D, C), :] = qp16[c * C:(c + 1) * C]
                n_s[hh, pl.ds(n0, D), :] = mn[i][c * D:(c + 1) * D, D:]
                gl_s[hh, pl.ds(g0, SUBLANES), :] = jnp.broadcast_to(
                    jnp.exp(g_last[i][c * C:c * C + SUBLANES, :]), (SUBLANES, D))
        return carry

    lax.fori_loop(0, seq // (P * pairs_per_step), pair_body, 0)

    ng = ng_ref[...]

    def rec_body(n, state):
        r0 = pl.multiple_of(n * C, C)
        n0 = pl.multiple_of(n * D, D)
        q0 = pl.multiple_of(n * (D + C), D + C)
        s16 = [s.astype(BF16) for s in state]
        ms = [_dot(mq_s[hh, pl.ds(q0, D + C), :], s16[hh]) for hh in heads]
        new_state = []
        for hh in heads:
            gl = gl_s[hh, pl.ds(pl.multiple_of(n * SUBLANES, SUBLANES), 1), :]
            new_state.append(state[hh] * gl + ms[hh][:D] + n_s[hh, pl.ds(n0, D), :])
        for hh in heads:
            cs = cols[hh]
            o = ms[hh][D:] + op_s[pl.ds(r0, C), cs]
            o = o * lax.rsqrt(jnp.mean(o * o, axis=-1, keepdims=True) + RMS_EPS) * ng
            z = z_ref[pl.ds(r0, C), cs].astype(F32)
            o_ref[pl.ds(r0, C), cs] = (o * _silu(z)).astype(o_ref.dtype)
        return tuple(new_state)

    lax.fori_loop(0, n_chunks, rec_body, tuple(jnp.zeros((D, D), F32) for _ in heads), unroll=2)


def _gdn(proj3, conv_w2, a_t, b_t, a_log, dt_bias, norm_g, batch, seq, col0, hb=2):
    C = GDN_CHUNK
    H = GDN_HEADS
    D = HEAD_DIM
    P = 2 * C
    bw = hb * D
    assert H % hb == 0 and col0 % bw == 0 and seq % 256 == 0 and P == LANES
    n_hg = H // hb
    c0 = col0 // bw
    nq = H * D // bw
    n_chunks = seq // C
    n_pairs = seq // P

    def pspec(seg):
        return pl.BlockSpec((None, seq, bw), lambda b, g, *_: (b, 0, c0 + seg * nq + g))

    def cwspec(seg):
        return pl.BlockSpec((GDN_CONV, bw), lambda b, g, *_: (0, seg * nq + g))

    abspec = pl.BlockSpec((None, hb, n_pairs, P), lambda b, g, *_: (b, g, 0, 0))
    kern = functools.partial(_gdn_kernel, hb=hb, seq=seq)
    return pl.pallas_call(
        kern,
        grid_spec=pltpu.PrefetchScalarGridSpec(
            num_scalar_prefetch=2,
            grid=(batch, n_hg),
            in_specs=[pspec(0), pspec(1), pspec(2), pspec(3), cwspec(0), cwspec(1), cwspec(2),
                      abspec, abspec, pl.BlockSpec((1, D), lambda b, g, *_: (0, 0))],
            out_specs=pl.BlockSpec((None, seq, bw), lambda b, g, *_: (b, 0, g)),
            scratch_shapes=[
                pltpu.VMEM((seq + SUBLANES, bw), F32),
                pltpu.VMEM((seq, bw), F32),
                pltpu.VMEM((seq, bw), F32),
                pltpu.VMEM((seq, bw), F32),
                pltpu.VMEM((hb, n_pairs, P), F32),
                pltpu.VMEM((hb, n_chunks * (D + C), D), BF16),
                pltpu.VMEM((seq, bw), F32),
                pltpu.VMEM((hb, n_chunks * D, D), F32),
                pltpu.VMEM((hb, n_chunks * SUBLANES, D), F32),
            ]),
        out_shape=jax.ShapeDtypeStruct((batch, seq, H * D), BF16),
        compiler_params=_cparams(("parallel", "parallel")),
        name="gdn",
    )(a_log, dt_bias, proj3, proj3, proj3, proj3, conv_w2, conv_w2, conv_w2, a_t, b_t, norm_g)


def _merge_kernel(o0_ref, o1_ref, o2_ref, l0_ref, l1_ref, l2_ref, yb_ref, ga_ref, gb_ref,
                  wa_ref, wb_ref, out_ref, ya_s):
    @pl.when(pl.program_id(1) == 0)
    def _():
        l0, l1, l2 = l0_ref[...], l1_ref[...], l2_ref[...]
        m = jnp.maximum(jnp.maximum(l0, l1), l2)
        e0, e1, e2 = jnp.exp(l0 - m), jnp.exp(l1 - m), jnp.exp(l2 - m)
        inv = 1.0 / (e0 + e1 + e2)
        w0, w1, w2 = e0 * inv, e1 * inv, e2 * inv
        for h in range(SWA_HEADS_PER_GROUP):
            cs = slice(h * HEAD_DIM, (h + 1) * HEAD_DIM)
            ya = (w0[:, h:h + 1] * o0_ref[:, cs].astype(F32) + w1[:, h:h + 1] * o1_ref[:, cs].astype(F32)
                  + w2[:, h:h + 1] * o2_ref[:, cs].astype(F32))
            ya_s[:, cs] = ya.astype(BF16)

    pa = _dot(ya_s[...], wa_ref[...])
    pb = _dot(yb_ref[...], wb_ref[...])
    out_ref[...] = (_sigmoid(ga_ref[...].astype(F32)) * pa
                    + _sigmoid(gb_ref[...].astype(F32)) * pb).astype(out_ref.dtype)


def _merge(o_groups, lse_groups, y_b, gates, w_o_swa, w_o_gdn, d_model, *, tm=512, tn=512):
    t = y_b.shape[0]
    wa_k, wb_k = w_o_swa.shape[0], w_o_gdn.shape[0]
    assert t % tm == 0 and d_model % tn == 0
    nb = d_model // tn
    row = lambda w: pl.BlockSpec((tm, w), lambda i, j: (i, 0))
    return pl.pallas_call(
        _merge_kernel,
        grid=(t // tm, nb),
        in_specs=[row(wa_k), row(wa_k), row(wa_k), row(LANES), row(LANES), row(LANES), row(wb_k),
                  pl.BlockSpec((tm, tn), lambda i, j: (i, j)),
                  pl.BlockSpec((tm, tn), lambda i, j: (i, nb + j)),
                  pl.BlockSpec((wa_k, tn), lambda i, j: (0, j)),
                  pl.BlockSpec((wb_k, tn), lambda i, j: (0, j))],
        out_specs=pl.BlockSpec((tm, tn), lambda i, j: (i, j)),
        out_shape=jax.ShapeDtypeStruct((t, d_model), BF16),
        scratch_shapes=[pltpu.VMEM((tm, wa_k), BF16)],
        compiler_params=_cparams(("parallel", "arbitrary")),
        name="merge",
    )(*o_groups, *lse_groups, y_b, gates, gates, w_o_swa, w_o_gdn)


def _layer_norm(v, g, b):
    mu = jnp.mean(v, axis=-1, keepdims=True)
    c = v - mu
    var = jnp.mean(c * c, axis=-1, keepdims=True)
    return c * lax.rsqrt(var + LN_EPS) * g + b


def _proj_ln_kernel(a_ref, w_ref, res_ref, g_ref, b_ref, o_ref, op_ref, *, tm):
    k = pl.program_id(1)
    part = _dot(a_ref[...], w_ref[...])

    @pl.when(k == 0)
    def _():
        o_ref[...] = part

    @pl.when(k > 0)
    def _():
        o_ref[...] += part

    @pl.when(k == pl.num_programs(1) - 1)
    def _():
        slab = 64
        for r in range(0, tm, slab):
            rs = slice(r, r + slab)
            hn = _layer_norm(DEEPNORM_ALPHA * res_ref[rs, :] + o_ref[rs, :], g_ref[...], b_ref[...])
            o_ref[rs, :] = hn
            packed = _pack_halves(hn)
            spt = packed.shape[1] // LANES
            for j in range(spt):
                op_ref[pl.ds(r * spt + j, slab, stride=spt), :] = packed[:, j * LANES:(j + 1) * LANES]


def _proj_ln(a, w, res, g, b, *, tm=512, tk=512):
    t, kdim = a.shape
    d = w.shape[1]
    assert t % tm == 0 and kdim % tk == 0
    return pl.pallas_call(
        functools.partial(_proj_ln_kernel, tm=tm),
        grid=(t // tm, kdim // tk),
        in_specs=[pl.BlockSpec((tm, tk), lambda i, k: (i, k)),
                  pl.BlockSpec((tk, d), lambda i, k: (k, 0)),
                  pl.BlockSpec((tm, d), lambda i, k: (i, 0), pipeline_mode=pl.Buffered(1)),
                  pl.BlockSpec((1, d), lambda i, k: (0, 0)),
                  pl.BlockSpec((1, d), lambda i, k: (0, 0))],
        out_specs=[pl.BlockSpec((tm, d), lambda i, k: (i, 0)),
                   pl.BlockSpec((tm * (d // 2 // LANES), LANES), lambda i, k: (i, 0))],
        out_shape=[jax.ShapeDtypeStruct((t, d), F32),
                   jax.ShapeDtypeStruct((t * (d // 2 // LANES), LANES), jnp.uint32)],
        compiler_params=_cparams(("parallel", "arbitrary")),
        name="proj_ln",
    )(a, w, res, g, b)


def _router_kernel(h_ref, w_ref, bias_ref, eidx_ref, gate_ref, pos_ref, cnt_ref, cnt_s, *, tm):
    i = pl.program_id(0)

    @pl.when(i == 0)
    def _():
        cnt_s[...] = jnp.zeros(cnt_s.shape, F32)

    scores = _sigmoid(_dot_hi(h_ref[...], w_ref[...]))
    lane = lax.broadcasted_iota(jnp.int32, (tm, LANES), 1)
    sel = jnp.where(lane < N_EXPERTS, scores + bias_ref[...], -jnp.inf)
    chosen = jnp.zeros((tm, LANES), jnp.bool_)
    eidx = jnp.zeros((tm, LANES), jnp.int32)
    gsel = jnp.zeros((tm, LANES), F32)
    picks = []
    for k in range(TOP_K):
        m = jnp.max(sel, axis=-1, keepdims=True)
        idx = jnp.min(jnp.where(sel == m, lane, LANES), axis=-1, keepdims=True)
        picks.append(idx)
        hit = lane == idx
        sc = jnp.sum(jnp.where(hit, scores, 0.0), axis=-1, keepdims=True)
        eidx = jnp.where(lane == k, idx, eidx)
        gsel = jnp.where(lane == k, sc, gsel)
        chosen = chosen | hit
        sel = jnp.where(hit, -jnp.inf, sel)
    gate_ref[...] = gsel / jnp.sum(gsel, axis=-1, keepdims=True) * ROUTED_SCALE

    ri = lax.broadcasted_iota(jnp.int32, (tm, tm), 0)
    ci = lax.broadcasted_iota(jnp.int32, (tm, tm), 1)
    ch = jnp.where(chosen, 1.0, 0.0)
    rank = _dot(jnp.where(ri > ci, 1.0, 0.0).astype(BF16), ch.astype(BF16)) + cnt_s[0:1, :]
    pos = jnp.zeros((tm, LANES), F32)
    for k in range(TOP_K):
        pk = jnp.sum(jnp.where(lane == picks[k], rank, 0.0), axis=-1, keepdims=True)
        pos = jnp.where(lane == k, pk, pos)
    pos_ref[...] = pos.astype(jnp.int32)
    eidx_ref[...] = eidx
    total = cnt_s[...] + jnp.sum(ch, axis=0, keepdims=True)
    cnt_s[...] = total
    cnt_ref[...] = total


def _router(h, w_router_p, bias_p, *, tm=256):
    t, d = h.shape
    assert t % tm == 0
    tile = pl.BlockSpec((tm, LANES), lambda i: (i, 0))
    return pl.pallas_call(
        functools.partial(_router_kernel, tm=tm),
        grid=(t // tm,),
        in_specs=[pl.BlockSpec((tm, d), lambda i: (i, 0)),
                  pl.BlockSpec((d, LANES), lambda i: (0, 0)),
                  pl.BlockSpec((1, LANES), lambda i: (0, 0))],
        out_specs=[tile, tile, tile, pl.BlockSpec((SUBLANES, LANES), lambda i: (0, 0))],
        out_shape=[jax.ShapeDtypeStruct((t, LANES), jnp.int32),
                   jax.ShapeDtypeStruct((t, LANES), F32),
                   jax.ShapeDtypeStruct((t, LANES), jnp.int32),
                   jax.ShapeDtypeStruct((SUBLANES, LANES), F32)],
        scratch_shapes=[pltpu.VMEM((SUBLANES, LANES), F32)],
        compiler_params=_cparams(("arbitrary",)),
        name="router",
    )(h, w_router_p, bias_p)


def _gather_copy(h_hbm, xbuf, sem, tok, slot, r):
    spt = h_hbm.shape[1]
    return pltpu.make_async_copy(h_hbm.at[tok], xbuf.at[slot, pl.ds(r * spt, spt), :], sem.at[slot])


def _stream_expert_weights(i, n_used, be_ref, next_ref, w_hbms, stage, w16s, wsem, wslot):
    last = jnp.maximum(n_used - 1, 0)
    cur = be_ref[jnp.minimum(i, last)]
    prev = be_ref[jnp.minimum(jnp.maximum(i - 1, 0), last)]
    active = i < n_used
    arrays = range(len(w_hbms))

    rows_per_copy = stage.shape[2] // WEIGHT_COPIES

    def copies(a, e, s):
        return [pltpu.make_async_copy(w_hbms[a].at[e, pl.ds(c * rows_per_copy, rows_per_copy), :],
                                      stage.at[s, a, pl.ds(c * rows_per_copy, rows_per_copy), :],
                                      wsem.at[s, a]) for c in range(WEIGHT_COPIES)]

    def start_all(e, s):
        for a in arrays:
            for cp in copies(a, e, s):
                cp.start()

    @pl.when(active & (i == 0))
    def _():
        wslot[0] = 0
        start_all(cur, 0)

    @pl.when(active & ((i == 0) | (cur != prev)))
    def _():
        s = wslot[0]
        for a in arrays:
            for cp in copies(a, cur, s):
                cp.wait()
        rows, cols = stage.shape[2], stage.shape[3]
        step = min(rows, max(SUBLANES, 256 * SUBLANES * LANES // cols))
        for a in arrays:
            for r in range(0, rows, step):
                w16s[a][r:r + step, :] = stage[s, a, r:r + step, :].astype(BF16)
        nxt = next_ref[cur]

        @pl.when(nxt >= 0)
        def _():
            start_all(nxt, 1 - s)
        wslot[0] = 1 - s


def _experts_up_kernel(be_ref, next_ref, tok_ref, nused_ref, h_hbm, wg_hbm, wu_hbm, hid_ref,
                       xbuf, stage, wg16, wu16, sem, wsem, wslot):
    i = pl.program_id(0)
    n_used = nused_ref[0]
    slot = lax.rem(i, 2)

    def wait_slot(s):
        def wbody(r, carry):
            _gather_copy(h_hbm, xbuf, sem, 0, s, r).wait()
            return carry
        lax.fori_loop(0, MOE_ROWS, wbody, 0, unroll=8)

    @pl.when((i == 0) & (n_used > 0))
    def _():
        def body(r, carry):
            _gather_copy(h_hbm, xbuf, sem, tok_ref[r], 0, r).start()
            return carry
        lax.fori_loop(0, MOE_ROWS, body, 0, unroll=8)

    _stream_expert_weights(i, n_used, be_ref, next_ref, (wg_hbm, wu_hbm), stage, (wg16, wu16), wsem, wslot)

    @pl.when(i < n_used)
    def _():
        base = jnp.minimum(i + 1, n_used - 1) * MOE_ROWS
        for r in range(MOE_ROWS):
            _gather_copy(h_hbm, xbuf, sem, tok_ref[base + r], 1 - slot, r).start()
        wait_slot(slot)
        spt = h_hbm.shape[1]
        parts = [_unpack_halves(xbuf[slot, pl.ds(j, MOE_ROWS, stride=spt), :]) for j in range(spt)]
        x = jnp.concatenate([p[0].astype(BF16) for p in parts] + [p[1].astype(BF16) for p in parts], axis=1)
        hid_ref[...] = (_silu(_dot(x, wg16[...])) * _dot(x, wu16[...])).astype(hid_ref.dtype)

    @pl.when(i == n_used - 1)
    def _():
        wait_slot(1 - slot)

    @pl.when(i >= n_used)
    def _():
        hid_ref[...] = jnp.zeros(hid_ref.shape, hid_ref.dtype)


def _experts_down_kernel(be_ref, next_ref, nused_ref, hid_ref, wd_hbm, y_ref, stage, wd16, wsem, wslot):
    i = pl.program_id(0)
    n_used = nused_ref[0]
    spt = y_ref.shape[0] // MOE_ROWS

    _stream_expert_weights(i, n_used, be_ref, next_ref, (wd_hbm,), stage, (wd16,), wsem, wslot)

    @pl.when(i < n_used)
    def _():
        packed = _pack_halves(_dot(hid_ref[...], wd16[...]))
        for j in range(spt):
            y_ref[pl.ds(j, MOE_ROWS, stride=spt), :] = packed[:, j * LANES:(j + 1) * LANES]

    @pl.when(i >= n_used)
    def _():
        y_ref[...] = jnp.zeros(y_ref.shape, y_ref.dtype)


def _experts(h, row_tok, block_e, next_e, n_used, wg, wu, wd):
    d = wg.shape[1]
    dh = d // 2
    spt = dh // LANES
    h_slabs = h.reshape(-1, spt, LANES)
    n_blocks = block_e.shape[0]
    de = wg.shape[2]
    rows = n_blocks * MOE_ROWS
    hbm = pl.BlockSpec(memory_space=pl.ANY)

    hid = pl.pallas_call(
        _experts_up_kernel,
        grid_spec=pltpu.PrefetchScalarGridSpec(
            num_scalar_prefetch=4,
            grid=(n_blocks,),
            in_specs=[hbm, hbm, hbm],
            out_specs=pl.BlockSpec((MOE_ROWS, de), lambda i, *_: (i, 0)),
            scratch_shapes=[pltpu.VMEM((2, MOE_ROWS * spt, LANES), jnp.uint32),
                            pltpu.VMEM((2, 2, d, de), F32),
                            pltpu.VMEM((d, de), BF16), pltpu.VMEM((d, de), BF16),
                            pltpu.SemaphoreType.DMA((2,)), pltpu.SemaphoreType.DMA((2, 2)),
                            pltpu.SMEM((1,), jnp.int32)]),
        out_shape=jax.ShapeDtypeStruct((rows, de), BF16),
        compiler_params=_cparams(("arbitrary",)),
        name="experts_up",
    )(block_e, next_e, row_tok, n_used, h_slabs, wg, wu)
    return pl.pallas_call(
        _experts_down_kernel,
        grid_spec=pltpu.PrefetchScalarGridSpec(
            num_scalar_prefetch=3,
            grid=(n_blocks,),
            in_specs=[pl.BlockSpec((MOE_ROWS, de), lambda i, *_: (i, 0)), hbm],
            out_specs=pl.BlockSpec((MOE_ROWS * spt, LANES), lambda i, *_: (i, 0)),
            scratch_shapes=[pltpu.VMEM((2, 1, de, d), F32), pltpu.VMEM((de, d), BF16),
                            pltpu.SemaphoreType.DMA((2, 1)), pltpu.SMEM((1,), jnp.int32)]),
        out_shape=jax.ShapeDtypeStruct((rows * spt, LANES), jnp.uint32),
        compiler_params=_cparams(("arbitrary",)),
        name="experts_down",
    )(block_e, next_e, n_used, hid, wd)


def _shared_kernel(h_ref, wg_ref, wu_ref, wd_ref, o_ref):
    x = h_ref[...].astype(BF16)
    hid = _silu(_dot(x, wg_ref[...])) * _dot(x, wu_ref[...])
    o_ref[...] = _dot(hid.astype(BF16), wd_ref[...])


def _shared(h, wg, wu, wd, *, tm=512):
    t, d = h.shape
    ds_ = wg.shape[1]
    return pl.pallas_call(
        _shared_kernel,
        grid=(t // tm,),
        in_specs=[pl.BlockSpec((tm, d), lambda i: (i, 0)),
                  pl.BlockSpec((d, ds_), lambda i: (0, 0)),
                  pl.BlockSpec((d, ds_), lambda i: (0, 0)),
                  pl.BlockSpec((ds_, d), lambda i: (0, 0))],
        out_specs=pl.BlockSpec((tm, d), lambda i: (i, 0)),
        out_shape=jax.ShapeDtypeStruct((t, d), F32),
        compiler_params=_cparams(("parallel",)),
        name="shared_expert",
    )(h, wg, wu, wd)


def _combine_copy(y_hbm, ybuf, sem, row, slot, k, r):
    spt = y_hbm.shape[1]
    return pltpu.make_async_copy(y_hbm.at[row], ybuf.at[slot, k, pl.ds(r * spt, spt), :], sem.at[slot])


def _combine_kernel(dest_ref, y_hbm, h_ref, sh_ref, gate_ref, g_ref, b_ref, o_ref, ybuf, sem, *, tm):
    i = pl.program_id(0)
    n = pl.num_programs(0)
    slot = lax.rem(i, 2)

    def issue(tile, s):
        def body(r, carry):
            base = (tile * tm + r) * TOP_K
            for k in range(TOP_K):
                _combine_copy(y_hbm, ybuf, sem, dest_ref[base + k], s, k, r).start()
            return carry
        lax.fori_loop(0, tm, body, 0, unroll=4)

    @pl.when(i == 0)
    def _():
        issue(0, 0)

    @pl.when(i + 1 < n)
    def _():
        issue(i + 1, 1 - slot)

    def wbody(r, carry):
        for k in range(TOP_K):
            _combine_copy(y_hbm, ybuf, sem, 0, slot, k, r).wait()
        return carry
    lax.fori_loop(0, tm, wbody, 0)

    slab = 32
    dh = h_ref.shape[1] // 2
    spt = y_hbm.shape[1]
    for r in range(0, tm, slab):
        rs = slice(r, r + slab)
        gate = gate_ref[rs, :]
        gk = [gate[:, k:k + 1] for k in range(TOP_K)]
        res = DEEPNORM_ALPHA * h_ref[rs, :] + sh_ref[rs, :]
        his, los = [], []
        for j in range(spt):
            acc_hi = res[:, j * LANES:(j + 1) * LANES]
            acc_lo = res[:, dh + j * LANES:dh + (j + 1) * LANES]
            for k in range(TOP_K):
                hi, lo = _unpack_halves(ybuf[slot, k, pl.ds(r * spt + j, slab, stride=spt), :])
                acc_hi = acc_hi + gk[k] * hi
                acc_lo = acc_lo + gk[k] * lo
            his.append(acc_hi)
            los.append(acc_lo)
        o_ref[rs, :] = _layer_norm(jnp.concatenate(his + los, axis=1), g_ref[...], b_ref[...])


def _combine(dest_flat, y_rows, h, shared, gate, g, b, *, tm=128):
    t, d = h.shape
    spt = d // 2 // LANES
    assert t % tm == 0
    y_rows = y_rows.reshape(-1, spt, LANES)
    return pl.pallas_call(
        functools.partial(_combine_kernel, tm=tm),
        grid_spec=pltpu.PrefetchScalarGridSpec(
            num_scalar_prefetch=1,
            grid=(t // tm,),
            in_specs=[pl.BlockSpec(memory_space=pl.ANY),
                      pl.BlockSpec((tm, d), lambda i, dr: (i, 0)),
                      pl.BlockSpec((tm, d), lambda i, dr: (i, 0)),
                      pl.BlockSpec((tm, LANES), lambda i, dr: (i, 0)),
                      pl.BlockSpec((1, d), lambda i, dr: (0, 0)),
                      pl.BlockSpec((1, d), lambda i, dr: (0, 0))],
            out_specs=pl.BlockSpec((tm, d), lambda i, dr: (i, 0)),
            scratch_shapes=[pltpu.VMEM((2, TOP_K, tm * spt, LANES), jnp.uint32),
                            pltpu.SemaphoreType.DMA((2,))]),
        out_shape=jax.ShapeDtypeStruct((t, d), F32),
        compiler_params=_cparams(("arbitrary",)),
        name="combine",
    )(dest_flat, y_rows, h, shared, gate, g, b)


def _layer(x, w_in, conv_w, a_log, dt_bias, norm_g, w_o_swa, w_o_gdn, w_out, ln1_g, ln1_b,
           w_router, router_bias, w_e_gate, w_e_up, w_e_down, w_s_gate, w_s_up, w_s_down, ln2_g, ln2_b):
    batch, seq, d = x.shape
    t = batch * seq
    n_groups = len(SWA_GROUPS)
    swa_w = n_groups * SWA_HEADS_PER_GROUP * HEAD_DIM
    gdn_w = GDN_HEADS * HEAD_DIM
    main_w = 3 * swa_w + 4 * gdn_w
    ba_w = 2 * GDN_HEADS
    assert w_in.shape[1] == main_w + ba_w + 2 * d

    xf = x.reshape(t, d)
    x16 = xf.astype(BF16)

    tn = 512
    gw = SWA_HEADS_PER_GROUP * HEAD_DIM
    assert gw % tn == 0 and (3 * swa_w) % tn == 0 and (4 * gdn_w) % tn == 0
    bpg = gw // tn

    w_in_t = w_in.T

    o_groups, lse_groups = [], []
    for j, (_, dil) in enumerate(SWA_GROUPS):
        col_map = lambda c, j=j: ((c // bpg) * n_groups + j) * bpg + c % bpg
        if dil == 1:
            qkv = _proj(x16, w_in_t, 3 * gw, tm=1024, tn=tn, out_dtype=BF16, col_map=col_map,
                        name=f"in_proj_swa{j}")
        else:
            qkv = _proj_residue(x16, w_in_t, 3 * gw, batch, seq, dil, tm=1024, tn=tn, out_dtype=BF16,
                                col_map=col_map, name=f"in_proj_swa{j}")
        o, lse = _swa_group(qkv, j, batch, seq)
        o_groups.append(o)
        lse_groups.append(lse)

    gdn_in = _proj(x16, w_in_t, 4 * gdn_w, tm=1024, tn=tn, out_dtype=BF16, name="in_proj_gdn",
                   col_map=lambda c: c + 3 * swa_w // tn)
    w_ba = jnp.pad(w_in_t[main_w:main_w + ba_w], ((0, LANES - ba_w), (0, 0)))
    ba = _proj(x16, w_ba, LANES, tm=1024, tn=LANES, out_dtype=F32, name="in_proj_ba")
    gates = _proj(x16, w_in_t[main_w + ba_w:], 2 * d, tm=1024, tn=tn, out_dtype=BF16, name="in_proj_gates")
    pair = 2 * GDN_CHUNK
    ba3 = ba[:, :ba_w].reshape(batch, seq // pair, pair, ba_w).transpose(0, 3, 1, 2)
    b_t, a_t = ba3[:, :GDN_HEADS], ba3[:, GDN_HEADS:]
    y_b = _gdn(gdn_in.reshape(batch, seq, 4 * gdn_w), conv_w.reshape(GDN_CONV, 3 * gdn_w), a_t, b_t,
               a_log.astype(F32), dt_bias.astype(F32), norm_g.reshape(1, HEAD_DIM).astype(F32), batch, seq, 0)
    y_b = y_b.reshape(t, gdn_w)

    merged = _merge(o_groups, lse_groups, y_b, gates, w_o_swa.astype(BF16), w_o_gdn.astype(BF16), d)
    h, h_packed = _proj_ln(merged, w_out.astype(BF16), xf, ln1_g.reshape(1, d), ln1_b.reshape(1, d))

    e_pad = LANES - N_EXPERTS
    eidx, gate, pos, cnt = _router(h, jnp.pad(w_router, ((0, 0), (0, e_pad))),
                                   jnp.pad(router_bias.reshape(1, N_EXPERTS), ((0, 0), (0, e_pad))))
    counts = cnt[0, :N_EXPERTS].astype(jnp.int32)
    padded = (counts + MOE_ROWS - 1) // MOE_ROWS * MOE_ROWS
    pend = jnp.cumsum(padded)
    pstart = pend - padded
    e_sel = eidx[:, :TOP_K, None] == jnp.arange(N_EXPERTS, dtype=jnp.int32)
    dest = jnp.sum(jnp.where(e_sel, pstart, 0), axis=-1) + pos[:, :TOP_K]
    n_blocks = (t * TOP_K + N_EXPERTS * (MOE_ROWS - 1) + MOE_ROWS - 1) // MOE_ROWS
    dest_flat = dest.reshape(-1)
    tok_of = jnp.arange(t * TOP_K, dtype=jnp.int32) // TOP_K
    row_tok = jnp.zeros((n_blocks * MOE_ROWS,), jnp.int32).at[dest_flat].set(tok_of)
    block_e = jnp.minimum(jnp.searchsorted(pend, jnp.arange(n_blocks, dtype=jnp.int32) * MOE_ROWS, side='right'),
                          N_EXPERTS - 1).astype(jnp.int32)
    n_used = (pend[-1:] // MOE_ROWS).astype(jnp.int32)
    e_ids = jnp.arange(N_EXPERTS, dtype=jnp.int32)
    first_owner = lax.cummin(jnp.where(padded > 0, e_ids, N_EXPERTS), reverse=True)
    next_e = jnp.concatenate([first_owner[1:], jnp.full((1,), N_EXPERTS, jnp.int32)])
    next_e = jnp.where(next_e >= N_EXPERTS, -1, next_e).astype(jnp.int32)
    y_rows = _experts(h_packed, row_tok, block_e, next_e, n_used, w_e_gate, w_e_up, w_e_down)
    shared = _shared(h, w_s_gate.astype(BF16), w_s_up.astype(BF16), w_s_down.astype(BF16))
    out = _combine(dest_flat, y_rows, h, shared, gate, ln2_g.reshape(1, d), ln2_b.reshape(1, d))
    return out.reshape(batch, seq, d)


@jax.jit
def kernel(x, w_in, conv_w, gdn_a_log, gdn_dt_bias, gdn_norm_g, w_o_swa, w_o_gdn, w_out, ln1_g, ln1_b,
           w_router, router_bias, w_e_gate, w_e_up, w_e_down, w_s_gate, w_s_up, w_s_down, ln2_g, ln2_b):
    for l in range(DEPTH):
        x = _layer(x, w_in[l], conv_w[l], gdn_a_log[l], gdn_dt_bias[l], gdn_norm_g[l], w_o_swa[l],
                   w_o_gdn[l], w_out[l], ln1_g[l], ln1_b[l], w_router[l], router_bias[l], w_e_gate[l],
                   w_e_up[l], w_e_down[l], w_s_gate[l], w_s_up[l], w_s_down[l], ln2_g[l], ln2_b[l])
    return x
```

```python
import functools
import math

import jax
import jax.numpy as jnp
from jax import lax
from jax.experimental import pallas as pl
from jax.experimental.pallas import tpu as pltpu

HEAD_DIM = 128
SWA_GROUPS = ((128, 1), (512, 4), (2048, 16))
SWA_HEADS_PER_GROUP = 8
GDN_HEADS = 16
GDN_CONV = 4
GDN_CHUNK = 64
N_EXPERTS = 96
TOP_K = 8
ROUTED_SCALE = 2.5
LN_EPS = 1e-5
RMS_EPS = 1e-6
DEPTH = 1
DEEPNORM_ALPHA = (2 * DEPTH) ** 0.25

LANES = 128
SUBLANES = 8
VMEM_LIMIT = 60 * 1024 * 1024
MOE_ROWS = 256
GATHER_SLOTS = 3

F32 = jnp.float32
BF16 = jnp.bfloat16
HIGHEST = lax.Precision.HIGHEST


def _cparams(sem):
    return pltpu.CompilerParams(dimension_semantics=sem, vmem_limit_bytes=VMEM_LIMIT)


def _dot(a, b):
    return jnp.dot(a, b, preferred_element_type=F32)


def _dot_nt(a, b):
    return lax.dot_general(a, b, (((1,), (1,)), ((), ())), preferred_element_type=F32)


def _dot_tn(a, b):
    return lax.dot_general(a, b, (((0,), (0,)), ((), ())), preferred_element_type=F32)


def _dot_hi(a, b):
    return jnp.dot(a, b, preferred_element_type=F32, precision=HIGHEST)


def _sigmoid(x):
    return 1.0 / (1.0 + jnp.exp(-x))


def _silu(x):
    return x * _sigmoid(x)


def _pack_halves(v):
    n = v.shape[1] // 2
    hi = pltpu.bitcast(v[:, :n].astype(BF16).astype(F32), jnp.uint32)
    lo = pltpu.bitcast(v[:, n:].astype(BF16).astype(F32), jnp.uint32)
    return hi | (lo >> 16)


def _unpack_halves(p):
    hi = pltpu.bitcast(p & jnp.uint32(0xFFFF0000), F32)
    lo = pltpu.bitcast(p << 16, F32)
    return hi, lo


def _proj_kernel(a_ref, bt_ref, o_ref):
    o_ref[...] = _dot_nt(a_ref[...].astype(BF16), bt_ref[...].astype(BF16)).astype(o_ref.dtype)


def _proj(a, bt, n_cols, *, tm, tn, out_dtype, col_map=lambda j: j, name="proj"):
    m, k = a.shape
    assert m % tm == 0 and n_cols % tn == 0 and bt.shape[1] == k
    return pl.pallas_call(
        _proj_kernel,
        grid=(m // tm, n_cols // tn),
        in_specs=[pl.BlockSpec((tm, k), lambda i, j: (i, 0)),
                  pl.BlockSpec((tn, k), lambda i, j: (col_map(j), 0))],
        out_specs=pl.BlockSpec((tm, tn), lambda i, j: (i, j)),
        out_shape=jax.ShapeDtypeStruct((m, n_cols), out_dtype),
        compiler_params=_cparams(("parallel", "arbitrary")),
        name=name,
    )(a, bt)


def _proj_residue_kernel(a_ref, bt_ref, o_ref, y_s, *, dil):
    y = _dot_nt(a_ref[...].astype(BF16), bt_ref[...].astype(BF16))
    rows = y_s.shape[1] // dil
    for c in range(y_s.shape[0]):
        cs = slice(c * LANES, (c + 1) * LANES)
        y_s[c] = y[:, cs]
        for r in range(dil):
            o_ref[r, :, cs] = y_s[c, pl.ds(r, rows, stride=dil), :].astype(o_ref.dtype)


def _proj_residue(a, bt, n_cols, batch, seq, dil, *, tm, tn, out_dtype, col_map, name):
    m, k = a.shape
    assert seq % tm == 0 and tm % (dil * 16) == 0 and n_cols % tn == 0 and bt.shape[1] == k
    spb = seq // tm
    out = pl.pallas_call(
        functools.partial(_proj_residue_kernel, dil=dil),
        grid=(m // tm, n_cols // tn),
        in_specs=[pl.BlockSpec((tm, k), lambda i, j: (i, 0)),
                  pl.BlockSpec((tn, k), lambda i, j: (col_map(j), 0))],
        out_specs=pl.BlockSpec((None, dil, tm // dil, tn), lambda i, j: (i // spb, 0, i % spb, j)),
        out_shape=jax.ShapeDtypeStruct((batch, dil, seq // dil, n_cols), out_dtype),
        scratch_shapes=[pltpu.VMEM((tn // LANES, tm, LANES), F32)],
        compiler_params=_cparams(("parallel", "arbitrary")),
        name=name,
    )(a, bt)
    return out.reshape(m, n_cols)


def _swa_kernel(q_ref, k_ref, v_ref, o_ref, lse_ref, *, n_heads, span, n_blocks):
    scale = HEAD_DIM ** -0.5
    qi = lax.broadcasted_iota(jnp.int32, (span, 2 * span), 0)
    ci = lax.broadcasted_iota(jnp.int32, (span, 2 * span), 1)
    mask_win = (ci >= qi) & (ci <= qi + span)
    qi0 = lax.broadcasted_iota(jnp.int32, (span, span), 0)
    ci0 = lax.broadcasted_iota(jnp.int32, (span, span), 1)
    mask_first = ci0 <= qi0
    lane = lax.broadcasted_iota(jnp.int32, (span, LANES), 1)

    def attend(q, kw, vw, mask):
        s = _dot_nt(q, kw) * scale
        s = jnp.where(mask, s, -jnp.inf)
        m = jnp.max(s, axis=-1, keepdims=True)
        p = jnp.exp(s - m)
        den = jnp.sum(p, axis=-1, keepdims=True)
        o = _dot(p.astype(BF16), vw) / den
        return o, m + jnp.log(den)

    def block(q0, k0, klen, mask):
        lse_acc = jnp.zeros((span, LANES), F32)
        for h in range(n_heads):
            cs = slice(h * HEAD_DIM, (h + 1) * HEAD_DIM)
            o, lse = attend(q_ref[pl.ds(q0, span), cs], k_ref[pl.ds(k0, klen), cs],
                            v_ref[pl.ds(k0, klen), cs], mask)
            o_ref[pl.ds(q0, span), cs] = o.astype(o_ref.dtype)
            lse_acc = jnp.where(lane == h, lse, lse_acc)
        lse_ref[pl.ds(q0, span), :] = lse_acc

    block(0, 0, span, mask_first)

    def body(n, carry):
        q0 = pl.multiple_of(n * span, span)
        k0 = pl.multiple_of((n - 1) * span, span)
        block(q0, k0, 2 * span, mask_win)
        return carry

    lax.fori_loop(1, n_blocks, body, 0)


def _swa_group(qkv, group, batch, seq):
    window, dil = SWA_GROUPS[group]
    span = window // dil
    sub_len = seq // dil
    assert seq % dil == 0 and sub_len % span == 0 and span % LANES == 0
    gw = SWA_HEADS_PER_GROUP * HEAD_DIM
    view = qkv.reshape(batch, dil, sub_len, 3 * gw)
    kern = functools.partial(_swa_kernel, n_heads=SWA_HEADS_PER_GROUP, span=span,
                             n_blocks=sub_len // span)

    def spec(seg):
        return pl.BlockSpec((None, None, sub_len, gw), lambda b, r: (b, r, 0, seg))

    o, lse = pl.pallas_call(
        kern,
        grid=(batch, dil),
        in_specs=[spec(0), spec(1), spec(2)],
        out_specs=[pl.BlockSpec((None, sub_len, gw), lambda b, r: (b, 0, r)),
                   pl.BlockSpec((None, sub_len, LANES), lambda b, r: (b, 0, r))],
        out_shape=[jax.ShapeDtypeStruct((batch, sub_len, dil * gw), BF16),
                   jax.ShapeDtypeStruct((batch, sub_len, dil * LANES), F32)],
        compiler_params=_cparams(("parallel", "parallel")),
        name=f"swa_g{group}",
    )(view, view, view)
    return o.reshape(batch * seq, gw), lse.reshape(batch * seq, LANES)


def _gdn_kernel(alog_ref, dtb_ref, q_ref, k_ref, v_ref, z_ref, cwq_ref, cwk_ref, cwv_ref,
                a_ref, b_ref, ng_ref, o_ref,
                xpad, qn, kn, vn, gcum_s, mq_s, op_s, n_s, gl_s, *, hb, seq):
    C = GDN_CHUNK
    P = 2 * C
    D = HEAD_DIM
    n_chunks = seq // C
    hg = pl.program_id(1)
    pad = SUBLANES

    xpad[0:pad, :] = jnp.zeros((pad, hb * D), F32)
    rows = 256
    for src_ref, cw_ref, dst, mode in ((q_ref, cwq_ref, qn, "q"), (k_ref, cwk_ref, kn, "k"),
                                       (v_ref, cwv_ref, vn, "v")):
        xpad[pad:pad + seq, :] = src_ref[...].astype(F32)
        cw = cw_ref[...]
        for t0 in range(0, seq, rows):
            acc = xpad[pad + t0:pad + t0 + rows, :] * cw[GDN_CONV - 1:GDN_CONV, :]
            for j in range(GDN_CONV - 1):
                sh = GDN_CONV - 1 - j
                acc = acc + xpad[pad + t0 - sh:pad + t0 - sh + rows, :] * cw[j:j + 1, :]
            y = _silu(acc)
            for hh in range(hb):
                cs = slice(hh * D, (hh + 1) * D)
                yh = y[:, cs]
                if mode != "v":
                    yh = yh * lax.rsqrt(jnp.sum(yh * yh, axis=-1, keepdims=True) + 1e-6)
                if mode == "q":
                    yh = yh * (D ** -0.5)
                dst[t0:t0 + rows, cs] = yh

    ri = lax.broadcasted_iota(jnp.int32, (P, P), 0)
    ci = lax.broadcasted_iota(jnp.int32, (P, P), 1)
    same = (ri // C) == (ci // C)
    tri = same & (ri >= ci)
    strict = same & (ri > ci)
    eye = ri == ci
    last = ci == (ri | (C - 1))
    cum_ones = (same & (ri <= ci)).astype(F32)
    eye_f = eye.astype(F32)
    first_chunk = lax.broadcasted_iota(jnp.int32, (P, D), 0) < C

    for hh in range(hb):
        head = hg * hb + hh
        sp_in = a_ref[hh] + dtb_ref[head]
        softplus = jnp.maximum(sp_in, 0.0) + jnp.log(1.0 + jnp.exp(-jnp.abs(sp_in)))
        gcum_s[hh] = _dot_hi(-jnp.exp(alog_ref[head]) * softplus, cum_ones)

    heads = range(hb)
    cols = [slice(hh * D, (hh + 1) * D) for hh in heads]

    pairs_per_step = 2
    items = [(g, hh) for g in range(pairs_per_step) for hh in heads]
    its = range(len(items))

    def pair_body(mb, carry):
        mp = [mb * pairs_per_step + g for g, _ in items]
        r0 = [pl.multiple_of(m * P, P) for m in mp]
        hd = [hh for _, hh in items]
        g_cum_r = [jnp.broadcast_to(gcum_s[hd[i], pl.ds(mp[i], 1), :], (P, P)) for i in its]
        beta_r = [jnp.broadcast_to(_sigmoid(b_ref[hd[i], pl.ds(mp[i], 1), :]), (P, P)) for i in its]
        q = [qn[pl.ds(r0[i], P), cols[hd[i]]] for i in its]
        k = [kn[pl.ds(r0[i], P), cols[hd[i]]] for i in its]
        v = [vn[pl.ds(r0[i], P), cols[hd[i]]] for i in its]
        g_cum_c = [jnp.sum(jnp.where(eye, g, 0.0), axis=1, keepdims=True) for g in g_cum_r]
        beta_c = [jnp.sum(jnp.where(eye, b, 0.0), axis=1, keepdims=True) for b in beta_r]
        g_last = [jnp.sum(jnp.where(last, g, 0.0), axis=1, keepdims=True) for g in g_cum_r]
        gam = [jnp.exp(jnp.where(tri, g_cum_c[i] - g_cum_r[i], -jnp.inf)) for i in its]
        kb = [k[i] * beta_c[i] for i in its]
        k16 = [x.astype(BF16) for x in k]
        aq = [_dot_nt(jnp.concatenate([kb[i].astype(BF16), q[i].astype(BF16)], axis=0), k16[i])
              for i in its]
        pw = [-jnp.where(strict, aq[i][:P] * gam[i], 0.0) for i in its]
        qk16 = [jnp.where(tri, aq[i][P:] * gam[i], 0.0).astype(BF16) for i in its]
        e_g = [jnp.exp(g) for g in g_cum_c]
        rhs16 = [jnp.concatenate([kb[i] * e_g[i], v[i] * beta_c[i]], axis=1).astype(BF16)
                 for i in its]
        kd = [k[i] * jnp.exp(g_last[i] - g_cum_c[i]) for i in its]
        t_inv = [eye_f + x for x in pw]
        p16 = [x.astype(BF16) for x in pw]
        pw = [_dot(x, x) for x in p16]
        for _ in range(int(math.log2(C)) - 2):
            p16 = [x.astype(BF16) for x in pw]
            tp = [_dot(jnp.concatenate([t_inv[i].astype(BF16), p16[i]], axis=0), p16[i]) for i in its]
            t_inv = [t_inv[i] + tp[i][:P] for i in its]
            pw = [tp[i][P:] for i in its]
        t_inv = [t_inv[i] + _dot(t_inv[i].astype(BF16), pw[i].astype(BF16)) for i in its]
        wu16 = [_dot(t_inv[i].astype(BF16), rhs16[i]).astype(BF16) for i in its]
        qo = [_dot(qk16[i], wu16[i]) for i in its]
        kd2 = [jnp.concatenate([jnp.where(first_chunk, x, 0.0), jnp.where(first_chunk, 0.0, x)],
                               axis=1).astype(BF16) for x in kd]
        mn = [_dot_tn(kd2[i], wu16[i]) for i in its]
        for i in its:
            hh = hd[i]
            op_s[pl.ds(r0[i], P), cols[hh]] = qo[i][:, D:]
            qp16 = (q[i] * e_g[i] - qo[i][:, :D]).astype(BF16)
            for c in range(P // C):
                chunk = mp[i] * (P // C) + c
                n0 = pl.multiple_of(chunk * D, D)
                q0 = pl.multiple_of(chunk * (D + C), D + C)
                g0 = pl.multiple_of(chunk * SUBLANES, SUBLANES)
                mq_s[hh, pl.ds(q0, D), :] = (-mn[i][c * D:(c + 1) * D, :D]).astype(BF16)
                mq_s[hh, pl.ds(q0 + D, C), :] = qp16[c * C:(c + 1) * C]
                n_s[hh, pl.ds(n0, D), :] = mn[i][c * D:(c + 1) * D, D:]
                gl_s[hh, pl.ds(g0, SUBLANES), :] = jnp.broadcast_to(
                    jnp.exp(g_last[i][c * C:c * C + SUBLANES, :]), (SUBLANES, D))
        return carry

    lax.fori_loop(0, seq // (P * pairs_per_step), pair_body, 0)

    ng = ng_ref[...]

    def rec_body(n, state):
        r0 = pl.multiple_of(n * C, C)
        n0 = pl.multiple_of(n * D, D)
        q0 = pl.multiple_of(n * (D + C), D + C)
        s16 = [s.astype(BF16) for s in state]
        ms = [_dot(mq_s[hh, pl.ds(q0, D + C), :], s16[hh]) for hh in heads]
        new_state = []
        for hh in heads:
            gl = gl_s[hh, pl.ds(pl.multiple_of(n * SUBLANES, SUBLANES), 1), :]
            new_state.append(state[hh] * gl + ms[hh][:D] + n_s[hh, pl.ds(n0, D), :])
        for hh in heads:
            cs = cols[hh]
            o = ms[hh][D:] + op_s[pl.ds(r0, C), cs]
            o = o * lax.rsqrt(jnp.mean(o * o, axis=-1, keepdims=True) + RMS_EPS) * ng
            z = z_ref[pl.ds(r0, C), cs].astype(F32)
            o_ref[pl.ds(r0, C), cs] = (o * _silu(z)).astype(o_ref.dtype)
        return tuple(new_state)

    lax.fori_loop(0, n_chunks, rec_body, tuple(jnp.zeros((D, D), F32) for _ in heads), unroll=2)


def _gdn(proj3, conv_w2, a_t, b_t, a_log, dt_bias, norm_g, batch, seq, col0, hb=2):
    C = GDN_CHUNK
    H = GDN_HEADS
    D = HEAD_DIM
    P = 2 * C
    bw = hb * D
    assert H % hb == 0 and col0 % bw == 0 and seq % 256 == 0 and P == LANES
    n_hg = H // hb
    c0 = col0 // bw
    nq = H * D // bw
    n_chunks = seq // C
    n_pairs = seq // P

    def pspec(seg):
        return pl.BlockSpec((None, seq, bw), lambda b, g, *_: (b, 0, c0 + seg * nq + g))

    def cwspec(seg):
        return pl.BlockSpec((GDN_CONV, bw), lambda b, g, *_: (0, seg * nq + g))

    abspec = pl.BlockSpec((None, hb, n_pairs, P), lambda b, g, *_: (b, g, 0, 0))
    kern = functools.partial(_gdn_kernel, hb=hb, seq=seq)
    return pl.pallas_call(
        kern,
        grid_spec=pltpu.PrefetchScalarGridSpec(
            num_scalar_prefetch=2,
            grid=(batch, n_hg),
            in_specs=[pspec(0), pspec(1), pspec(2), pspec(3), cwspec(0), cwspec(1), cwspec(2),
                      abspec, abspec, pl.BlockSpec((1, D), lambda b, g, *_: (0, 0))],
            out_specs=pl.BlockSpec((None, seq, bw), lambda b, g, *_: (b, 0, g)),
            scratch_shapes=[
                pltpu.VMEM((seq + SUBLANES, bw), F32),
                pltpu.VMEM((seq, bw), F32),
                pltpu.VMEM((seq, bw), F32),
                pltpu.VMEM((seq, bw), F32),
                pltpu.VMEM((hb, n_pairs, P), F32),
                pltpu.VMEM((hb, n_chunks * (D + C), D), BF16),
                pltpu.VMEM((seq, bw), F32),
                pltpu.VMEM((hb, n_chunks * D, D), F32),
                pltpu.VMEM((hb, n_chunks * SUBLANES, D), F32),
            ]),
        out_shape=jax.ShapeDtypeStruct((batch, seq, H * D), BF16),
        compiler_params=_cparams(("parallel", "parallel")),
        name="gdn",
    )(a_log, dt_bias, proj3, proj3, proj3, proj3, conv_w2, conv_w2, conv_w2, a_t, b_t, norm_g)


def _merge_kernel(o0_ref, o1_ref, o2_ref, l0_ref, l1_ref, l2_ref, yb_ref, ga_ref, gb_ref,
                  wa_ref, wb_ref, out_ref, ya_s):
    @pl.when(pl.program_id(1) == 0)
    def _():
        l0, l1, l2 = l0_ref[...], l1_ref[...], l2_ref[...]
        m = jnp.maximum(jnp.maximum(l0, l1), l2)
        e0, e1, e2 = jnp.exp(l0 - m), jnp.exp(l1 - m), jnp.exp(l2 - m)
        inv = 1.0 / (e0 + e1 + e2)
        w0, w1, w2 = e0 * inv, e1 * inv, e2 * inv
        for h in range(SWA_HEADS_PER_GROUP):
            cs = slice(h * HEAD_DIM, (h + 1) * HEAD_DIM)
            ya = (w0[:, h:h + 1] * o0_ref[:, cs].astype(F32) + w1[:, h:h + 1] * o1_ref[:, cs].astype(F32)
                  + w2[:, h:h + 1] * o2_ref[:, cs].astype(F32))
            ya_s[:, cs] = ya.astype(BF16)

    pa = _dot(ya_s[...], wa_ref[...])
    pb = _dot(yb_ref[...], wb_ref[...])
    out_ref[...] = (_sigmoid(ga_ref[...].astype(F32)) * pa
                    + _sigmoid(gb_ref[...].astype(F32)) * pb).astype(out_ref.dtype)


def _merge(o_groups, lse_groups, y_b, gates, w_o_swa, w_o_gdn, d_model, *, tm=512, tn=512):
    t = y_b.shape[0]
    wa_k, wb_k = w_o_swa.shape[0], w_o_gdn.shape[0]
    assert t % tm == 0 and d_model % tn == 0
    nb = d_model // tn
    row = lambda w: pl.BlockSpec((tm, w), lambda i, j: (i, 0))
    return pl.pallas_call(
        _merge_kernel,
        grid=(t // tm, nb),
        in_specs=[row(wa_k), row(wa_k), row(wa_k), row(LANES), row(LANES), row(LANES), row(wb_k),
                  pl.BlockSpec((tm, tn), lambda i, j: (i, j)),
                  pl.BlockSpec((tm, tn), lambda i, j: (i, nb + j)),
                  pl.BlockSpec((wa_k, tn), lambda i, j: (0, j)),
                  pl.BlockSpec((wb_k, tn), lambda i, j: (0, j))],
        out_specs=pl.BlockSpec((tm, tn), lambda i, j: (i, j)),
        out_shape=jax.ShapeDtypeStruct((t, d_model), BF16),
        scratch_shapes=[pltpu.VMEM((tm, wa_k), BF16)],
        compiler_params=_cparams(("parallel", "arbitrary")),
        name="merge",
    )(*o_groups, *lse_groups, y_b, gates, gates, w_o_swa, w_o_gdn)


def _layer_norm(v, g, b):
    mu = jnp.mean(v, axis=-1, keepdims=True)
    c = v - mu
    var = jnp.mean(c * c, axis=-1, keepdims=True)
    return c * lax.rsqrt(var + LN_EPS) * g + b


def _proj_ln_kernel(a_ref, w_ref, res_ref, g_ref, b_ref, o_ref, op_ref, *, tm):
    k = pl.program_id(1)
    part = _dot(a_ref[...], w_ref[...])

    @pl.when(k == 0)
    def _():
        o_ref[...] = part

    @pl.when(k > 0)
    def _():
        o_ref[...] += part

    @pl.when(k == pl.num_programs(1) - 1)
    def _():
        slab = 64
        for r in range(0, tm, slab):
            rs = slice(r, r + slab)
            hn = _layer_norm(DEEPNORM_ALPHA * res_ref[rs, :] + o_ref[rs, :], g_ref[...], b_ref[...])
            o_ref[rs, :] = hn
            op_ref[rs, :] = _pack_halves(hn)


def _proj_ln(a, w, res, g, b, *, tm=512, tk=512):
    t, kdim = a.shape
    d = w.shape[1]
    assert t % tm == 0 and kdim % tk == 0
    return pl.pallas_call(
        functools.partial(_proj_ln_kernel, tm=tm),
        grid=(t // tm, kdim // tk),
        in_specs=[pl.BlockSpec((tm, tk), lambda i, k: (i, k)),
                  pl.BlockSpec((tk, d), lambda i, k: (k, 0)),
                  pl.BlockSpec((tm, d), lambda i, k: (i, 0), pipeline_mode=pl.Buffered(1)),
                  pl.BlockSpec((1, d), lambda i, k: (0, 0)),
                  pl.BlockSpec((1, d), lambda i, k: (0, 0))],
        out_specs=[pl.BlockSpec((tm, d), lambda i, k: (i, 0)),
                   pl.BlockSpec((tm, d // 2), lambda i, k: (i, 0))],
        out_shape=[jax.ShapeDtypeStruct((t, d), F32),
                   jax.ShapeDtypeStruct((t, d // 2), jnp.uint32)],
        compiler_params=_cparams(("parallel", "arbitrary")),
        name="proj_ln",
    )(a, w, res, g, b)


def _router_kernel(h_ref, w_ref, bias_ref, eidx_ref, gate_ref, pos_ref, cnt_ref, cnt_s, *, tm):
    i = pl.program_id(0)

    @pl.when(i == 0)
    def _():
        cnt_s[...] = jnp.zeros(cnt_s.shape, F32)

    scores = _sigmoid(_dot_hi(h_ref[...], w_ref[...]))
    lane = lax.broadcasted_iota(jnp.int32, (tm, LANES), 1)
    sel = jnp.where(lane < N_EXPERTS, scores + bias_ref[...], -jnp.inf)
    chosen = jnp.zeros((tm, LANES), jnp.bool_)
    eidx = jnp.zeros((tm, LANES), jnp.int32)
    gsel = jnp.zeros((tm, LANES), F32)
    picks = []
    for k in range(TOP_K):
        m = jnp.max(sel, axis=-1, keepdims=True)
        idx = jnp.min(jnp.where(sel == m, lane, LANES), axis=-1, keepdims=True)
        picks.append(idx)
        hit = lane == idx
        sc = jnp.sum(jnp.where(hit, scores, 0.0), axis=-1, keepdims=True)
        eidx = jnp.where(lane == k, idx, eidx)
        gsel = jnp.where(lane == k, sc, gsel)
        chosen = chosen | hit
        sel = jnp.where(hit, -jnp.inf, sel)
    gate_ref[...] = gsel / jnp.sum(gsel, axis=-1, keepdims=True) * ROUTED_SCALE

    ri = lax.broadcasted_iota(jnp.int32, (tm, tm), 0)
    ci = lax.broadcasted_iota(jnp.int32, (tm, tm), 1)
    ch = jnp.where(chosen, 1.0, 0.0)
    rank = _dot(jnp.where(ri > ci, 1.0, 0.0).astype(BF16), ch.astype(BF16)) + cnt_s[0:1, :]
    pos = jnp.zeros((tm, LANES), F32)
    for k in range(TOP_K):
        pk = jnp.sum(jnp.where(lane == picks[k], rank, 0.0), axis=-1, keepdims=True)
        pos = jnp.where(lane == k, pk, pos)
    pos_ref[...] = pos.astype(jnp.int32)
    eidx_ref[...] = eidx
    total = cnt_s[...] + jnp.sum(ch, axis=0, keepdims=True)
    cnt_s[...] = total
    cnt_ref[...] = total


def _router(h, w_router_p, bias_p, *, tm=256):
    t, d = h.shape
    assert t % tm == 0
    tile = pl.BlockSpec((tm, LANES), lambda i: (i, 0))
    return pl.pallas_call(
        functools.partial(_router_kernel, tm=tm),
        grid=(t // tm,),
        in_specs=[pl.BlockSpec((tm, d), lambda i: (i, 0)),
                  pl.BlockSpec((d, LANES), lambda i: (0, 0)),
                  pl.BlockSpec((1, LANES), lambda i: (0, 0))],
        out_specs=[tile, tile, tile, pl.BlockSpec((SUBLANES, LANES), lambda i: (0, 0))],
        out_shape=[jax.ShapeDtypeStruct((t, LANES), jnp.int32),
                   jax.ShapeDtypeStruct((t, LANES), F32),
                   jax.ShapeDtypeStruct((t, LANES), jnp.int32),
                   jax.ShapeDtypeStruct((SUBLANES, LANES), F32)],
        scratch_shapes=[pltpu.VMEM((SUBLANES, LANES), F32)],
        compiler_params=_cparams(("arbitrary",)),
        name="router",
    )(h, w_router_p, bias_p)


def _gather_copy(h_hbm, xbuf, sem, tok, slot, r):
    return pltpu.make_async_copy(h_hbm.at[pl.ds(tok, 1), :], xbuf.at[slot, pl.ds(r, 1), :], sem.at[slot])


def _stream_expert_weights(i, n_used, be_ref, next_ref, w_hbms, stage, w16s, wsem, wslot):
    last = jnp.maximum(n_used - 1, 0)
    cur = be_ref[jnp.minimum(i, last)]
    prev = be_ref[jnp.minimum(jnp.maximum(i - 1, 0), last)]
    active = i < n_used
    arrays = range(len(w_hbms))

    def copy(a, e, s):
        return pltpu.make_async_copy(w_hbms[a].at[e], stage.at[s, a], wsem.at[s, a])

    @pl.when(active & (i == 0))
    def _():
        wslot[0] = 0
        for a in arrays:
            copy(a, cur, 0).start()

    @pl.when(active & ((i == 0) | (cur != prev)))
    def _():
        s = wslot[0]
        for a in arrays:
            copy(a, cur, s).wait()
        for a in arrays:
            w16s[a][...] = stage[s, a].astype(BF16)
        nxt = next_ref[cur]

        @pl.when(nxt >= 0)
        def _():
            for a in arrays:
                copy(a, nxt, 1 - s).start()
        wslot[0] = 1 - s


def _experts_up_kernel(be_ref, next_ref, tok_ref, nused_ref, h_hbm, wg_hbm, wu_hbm, hid_ref,
                       xbuf, stage, wg16, wu16, sem, wsem, wslot):
    i = pl.program_id(0)
    n_used = nused_ref[0]
    slot = lax.rem(i, GATHER_SLOTS)
    slot_p1 = lax.rem(i + 1, GATHER_SLOTS)
    slot_p2 = lax.rem(i + 2, GATHER_SLOTS)

    def wait_slot(s):
        def wbody(r, carry):
            _gather_copy(h_hbm, xbuf, sem, 0, s, r).wait()
            return carry
        lax.fori_loop(0, MOE_ROWS, wbody, 0, unroll=8)

    @pl.when((i == 0) & (n_used > 0))
    def _():
        second = jnp.minimum(1, n_used - 1) * MOE_ROWS

        def body(r, carry):
            _gather_copy(h_hbm, xbuf, sem, tok_ref[r], 0, r).start()
            _gather_copy(h_hbm, xbuf, sem, tok_ref[second + r], 1, r).start()
            return carry
        lax.fori_loop(0, MOE_ROWS, body, 0, unroll=8)

    _stream_expert_weights(i, n_used, be_ref, next_ref, (wg_hbm, wu_hbm), stage, (wg16, wu16), wsem, wslot)

    @pl.when(i < n_used)
    def _():
        base = jnp.minimum(i + 2, n_used - 1) * MOE_ROWS
        for r in range(MOE_ROWS):
            _gather_copy(h_hbm, xbuf, sem, tok_ref[base + r], slot_p2, r).start()
        wait_slot(slot)
        hi, lo = _unpack_halves(xbuf[slot])
        x = jnp.concatenate([hi.astype(BF16), lo.astype(BF16)], axis=1)
        hid_ref[...] = (_silu(_dot(x, wg16[...])) * _dot(x, wu16[...])).astype(hid_ref.dtype)

    @pl.when(i == n_used - 1)
    def _():
        wait_slot(slot_p1)
        wait_slot(slot_p2)

    @pl.when(i >= n_used)
    def _():
        hid_ref[...] = jnp.zeros(hid_ref.shape, hid_ref.dtype)


def _experts_down_kernel(be_ref, next_ref, nused_ref, hid_ref, wd_hbm, y_ref, stage, wd16, wsem, wslot):
    i = pl.program_id(0)
    n_used = nused_ref[0]

    _stream_expert_weights(i, n_used, be_ref, next_ref, (wd_hbm,), stage, (wd16,), wsem, wslot)

    @pl.when(i < n_used)
    def _():
        y_ref[...] = _pack_halves(_dot(hid_ref[...], wd16[...]))

    @pl.when(i >= n_used)
    def _():
        y_ref[...] = jnp.zeros(y_ref.shape, y_ref.dtype)


def _experts(h, row_tok, block_e, next_e, n_used, wg, wu, wd):
    t, dh = h.shape
    d = 2 * dh
    n_blocks = block_e.shape[0]
    de = wg.shape[2]
    rows = n_blocks * MOE_ROWS
    hbm = pl.BlockSpec(memory_space=pl.ANY)

    hid = pl.pallas_call(
        _experts_up_kernel,
        grid_spec=pltpu.PrefetchScalarGridSpec(
            num_scalar_prefetch=4,
            grid=(n_blocks,),
            in_specs=[hbm, hbm, hbm],
            out_specs=pl.BlockSpec((MOE_ROWS, de), lambda i, *_: (i, 0)),
            scratch_shapes=[pltpu.VMEM((GATHER_SLOTS, MOE_ROWS, dh), jnp.uint32),
                            pltpu.VMEM((2, 2, d, de), F32),
                            pltpu.VMEM((d, de), BF16), pltpu.VMEM((d, de), BF16),
                            pltpu.SemaphoreType.DMA((GATHER_SLOTS,)), pltpu.SemaphoreType.DMA((2, 2)),
                            pltpu.SMEM((1,), jnp.int32)]),
        out_shape=jax.ShapeDtypeStruct((rows, de), BF16),
        compiler_params=_cparams(("arbitrary",)),
        name="experts_up",
    )(block_e, next_e, row_tok, n_used, h, wg, wu)
    return pl.pallas_call(
        _experts_down_kernel,
        grid_spec=pltpu.PrefetchScalarGridSpec(
            num_scalar_prefetch=3,
            grid=(n_blocks,),
            in_specs=[pl.BlockSpec((MOE_ROWS, de), lambda i, *_: (i, 0)), hbm],
            out_specs=pl.BlockSpec((MOE_ROWS, dh), lambda i, *_: (i, 0)),
            scratch_shapes=[pltpu.VMEM((2, 1, de, d), F32), pltpu.VMEM((de, d), BF16),
                            pltpu.SemaphoreType.DMA((2, 1)), pltpu.SMEM((1,), jnp.int32)]),
        out_shape=jax.ShapeDtypeStruct((rows, dh), jnp.uint32),
        compiler_params=_cparams(("arbitrary",)),
        name="experts_down",
    )(block_e, next_e, n_used, hid, wd)


def _shared_kernel(h_ref, wg_ref, wu_ref, wd_ref, o_ref):
    x = h_ref[...].astype(BF16)
    hid = _silu(_dot(x, wg_ref[...])) * _dot(x, wu_ref[...])
    o_ref[...] = _dot(hid.astype(BF16), wd_ref[...])


def _shared(h, wg, wu, wd, *, tm=512):
    t, d = h.shape
    ds_ = wg.shape[1]
    return pl.pallas_call(
        _shared_kernel,
        grid=(t // tm,),
        in_specs=[pl.BlockSpec((tm, d), lambda i: (i, 0)),
                  pl.BlockSpec((d, ds_), lambda i: (0, 0)),
                  pl.BlockSpec((d, ds_), lambda i: (0, 0)),
                  pl.BlockSpec((ds_, d), lambda i: (0, 0))],
        out_specs=pl.BlockSpec((tm, d), lambda i: (i, 0)),
        out_shape=jax.ShapeDtypeStruct((t, d), F32),
        compiler_params=_cparams(("parallel",)),
        name="shared_expert",
    )(h, wg, wu, wd)


def _combine_copy(y_hbm, ybuf, sem, row, slot, k, r):
    return pltpu.make_async_copy(y_hbm.at[pl.ds(row, 1), :], ybuf.at[slot, k, pl.ds(r, 1), :], sem.at[slot])


def _combine_kernel(dest_ref, y_hbm, h_ref, sh_ref, gate_ref, g_ref, b_ref, o_ref, ybuf, sem, *, tm):
    i = pl.program_id(0)
    n = pl.num_programs(0)
    slot = lax.rem(i, 2)

    def issue(tile, s):
        def body(r, carry):
            base = (tile * tm + r) * TOP_K
            for k in range(TOP_K):
                _combine_copy(y_hbm, ybuf, sem, dest_ref[base + k], s, k, r).start()
            return carry
        lax.fori_loop(0, tm, body, 0, unroll=4)

    @pl.when(i == 0)
    def _():
        issue(0, 0)

    @pl.when(i + 1 < n)
    def _():
        issue(i + 1, 1 - slot)

    def wbody(r, carry):
        for k in range(TOP_K):
            _combine_copy(y_hbm, ybuf, sem, 0, slot, k, r).wait()
        return carry
    lax.fori_loop(0, tm, wbody, 0)

    slab = 32
    dh = h_ref.shape[1] // 2
    for r in range(0, tm, slab):
        rs = slice(r, r + slab)
        gate = gate_ref[rs, :]
        res = DEEPNORM_ALPHA * h_ref[rs, :] + sh_ref[rs, :]
        acc_hi, acc_lo = res[:, :dh], res[:, dh:]
        for k in range(TOP_K):
            hi, lo = _unpack_halves(ybuf[slot, k, rs, :])
            acc_hi = acc_hi + gate[:, k:k + 1] * hi
            acc_lo = acc_lo + gate[:, k:k + 1] * lo
        o_ref[rs, :] = _layer_norm(jnp.concatenate([acc_hi, acc_lo], axis=1), g_ref[...], b_ref[...])


def _combine(dest_flat, y_rows, h, shared, gate, g, b, *, tm=128):
    t, d = h.shape
    assert t % tm == 0
    return pl.pallas_call(
        functools.partial(_combine_kernel, tm=tm),
        grid_spec=pltpu.PrefetchScalarGridSpec(
            num_scalar_prefetch=1,
            grid=(t // tm,),
            in_specs=[pl.BlockSpec(memory_space=pl.ANY),
                      pl.BlockSpec((tm, d), lambda i, dr: (i, 0)),
                      pl.BlockSpec((tm, d), lambda i, dr: (i, 0)),
                      pl.BlockSpec((tm, LANES), lambda i, dr: (i, 0)),
                      pl.BlockSpec((1, d), lambda i, dr: (0, 0)),
                      pl.BlockSpec((1, d), lambda i, dr: (0, 0))],
            out_specs=pl.BlockSpec((tm, d), lambda i, dr: (i, 0)),
            scratch_shapes=[pltpu.VMEM((2, TOP_K, tm, d // 2), jnp.uint32), pltpu.SemaphoreType.DMA((2,))]),
        out_shape=jax.ShapeDtypeStruct((t, d), F32),
        compiler_params=_cparams(("arbitrary",)),
        name="combine",
    )(dest_flat, y_rows, h, shared, gate, g, b)


def _layer(x, w_in, conv_w, a_log, dt_bias, norm_g, w_o_swa, w_o_gdn, w_out, ln1_g, ln1_b,
           w_router, router_bias, w_e_gate, w_e_up, w_e_down, w_s_gate, w_s_up, w_s_down, ln2_g, ln2_b):
    batch, seq, d = x.shape
    t = batch * seq
    n_groups = len(SWA_GROUPS)
    swa_w = n_groups * SWA_HEADS_PER_GROUP * HEAD_DIM
    gdn_w = GDN_HEADS * HEAD_DIM
    main_w = 3 * swa_w + 4 * gdn_w
    ba_w = 2 * GDN_HEADS
    assert w_in.shape[1] == main_w + ba_w + 2 * d

    xf = x.reshape(t, d)
    x16 = xf.astype(BF16)

    tn = 512
    gw = SWA_HEADS_PER_GROUP * HEAD_DIM
    assert gw % tn == 0 and (3 * swa_w) % tn == 0 and (4 * gdn_w) % tn == 0
    bpg = gw // tn

    w_in_t = w_in.T

    o_groups, lse_groups = [], []
    for j, (_, dil) in enumerate(SWA_GROUPS):
        col_map = lambda c, j=j: ((c // bpg) * n_groups + j) * bpg + c % bpg
        if dil == 1:
            qkv = _proj(x16, w_in_t, 3 * gw, tm=1024, tn=tn, out_dtype=BF16, col_map=col_map,
                        name=f"in_proj_swa{j}")
        else:
            qkv = _proj_residue(x16, w_in_t, 3 * gw, batch, seq, dil, tm=1024, tn=tn, out_dtype=BF16,
                                col_map=col_map, name=f"in_proj_swa{j}")
        o, lse = _swa_group(qkv, j, batch, seq)
        o_groups.append(o)
        lse_groups.append(lse)

    gdn_in = _proj(x16, w_in_t, 4 * gdn_w, tm=1024, tn=tn, out_dtype=BF16, name="in_proj_gdn",
                   col_map=lambda c: c + 3 * swa_w // tn)
    w_ba = jnp.pad(w_in_t[main_w:main_w + ba_w], ((0, LANES - ba_w), (0, 0)))
    ba = _proj(x16, w_ba, LANES, tm=1024, tn=LANES, out_dtype=F32, name="in_proj_ba")
    gates = _proj(x16, w_in_t[main_w + ba_w:], 2 * d, tm=1024, tn=tn, out_dtype=BF16, name="in_proj_gates")
    pair = 2 * GDN_CHUNK
    ba3 = ba[:, :ba_w].reshape(batch, seq // pair, pair, ba_w).transpose(0, 3, 1, 2)
    b_t, a_t = ba3[:, :GDN_HEADS], ba3[:, GDN_HEADS:]
    y_b = _gdn(gdn_in.reshape(batch, seq, 4 * gdn_w), conv_w.reshape(GDN_CONV, 3 * gdn_w), a_t, b_t,
               a_log.astype(F32), dt_bias.astype(F32), norm_g.reshape(1, HEAD_DIM).astype(F32), batch, seq, 0)
    y_b = y_b.reshape(t, gdn_w)

    merged = _merge(o_groups, lse_groups, y_b, gates, w_o_swa.astype(BF16), w_o_gdn.astype(BF16), d)
    h, h_packed = _proj_ln(merged, w_out.astype(BF16), xf, ln1_g.reshape(1, d), ln1_b.reshape(1, d))

    e_pad = LANES - N_EXPERTS
    eidx, gate, pos, cnt = _router(h, jnp.pad(w_router, ((0, 0), (0, e_pad))),
                                   jnp.pad(router_bias.reshape(1, N_EXPERTS), ((0, 0), (0, e_pad))))
    counts = cnt[0, :N_EXPERTS].astype(jnp.int32)
    padded = (counts + MOE_ROWS - 1) // MOE_ROWS * MOE_ROWS
    pend = jnp.cumsum(padded)
    pstart = pend - padded
    e_sel = eidx[:, :TOP_K, None] == jnp.arange(N_EXPERTS, dtype=jnp.int32)
    dest = jnp.sum(jnp.where(e_sel, pstart, 0), axis=-1) + pos[:, :TOP_K]
    n_blocks = (t * TOP_K + N_EXPERTS * (MOE_ROWS - 1) + MOE_ROWS - 1) // MOE_ROWS
    dest_flat = dest.reshape(-1)
    tok_of = jnp.arange(t * TOP_K, dtype=jnp.int32) // TOP_K
    row_tok = jnp.zeros((n_blocks * MOE_ROWS,), jnp.int32).at[dest_flat].set(tok_of)
    block_e = jnp.minimum(jnp.searchsorted(pend, jnp.arange(n_blocks, dtype=jnp.int32) * MOE_ROWS, side='right'),
                          N_EXPERTS - 1).astype(jnp.int32)
    n_used = (pend[-1:] // MOE_ROWS).astype(jnp.int32)
    e_ids = jnp.arange(N_EXPERTS, dtype=jnp.int32)
    first_owner = lax.cummin(jnp.where(padded > 0, e_ids, N_EXPERTS), reverse=True)
    next_e = jnp.concatenate([first_owner[1:], jnp.full((1,), N_EXPERTS, jnp.int32)])
    next_e = jnp.where(next_e >= N_EXPERTS, -1, next_e).astype(jnp.int32)
    y_rows = _experts(h_packed, row_tok, block_e, next_e, n_used, w_e_gate, w_e_up, w_e_down)
    shared = _shared(h, w_s_gate.astype(BF16), w_s_up.astype(BF16), w_s_down.astype(BF16))
    out = _combine(dest_flat, y_rows, h, shared, gate, ln2_g.reshape(1, d), ln2_b.reshape(1, d))
    return out.reshape(batch, seq, d)


@jax.jit
def kernel(x, w_in, conv_w, gdn_a_log, gdn_dt_bias, gdn_norm_g, w_o_swa, w_o_gdn, w_out, ln1_g, ln1_b,
           w_router, router_bias, w_e_gate, w_e_up, w_e_down, w_s_gate, w_s_up, w_s_down, ln2_g, ln2_b):
    for l in range(DEPTH):
        x = _layer(x, w_in[l], conv_w[l], gdn_a_log[l], gdn_dt_bias[l], gdn_norm_g[l], w_o_swa[l],
                   w_o_gdn[l], w_out[l], ln1_g[l], ln1_b[l], w_router[l], router_bias[l], w_e_gate[l],
                   w_e_up[l], w_e_down[l], w_s_gate[l], w_s_up[l], w_s_down[l], ln2_g[l], ln2_b[l])
    return x
```
